```python
import math
import jax
import jax.numpy as jnp
from jax import lax
import numpy as np

D_MODEL = 1024
BATCH = 4
SEQ = 4096
DEPTH = 4
DEC_BATCH = 128
DEC_SEQ = 4
PAST_LEN = 8192
PAGE_SIZE = 128

SSM_WIDTH = D_MODEL // 2
SSM_GROUP = 16
SSM_GROUPS = SSM_WIDTH // SSM_GROUP
SSM_STATE = 64
SSM_DT_MIN = 0.001
SSM_DT_MAX = 0.1
MLA_HEADS = 8
MLA_Q_LORA = D_MODEL // 4
MLA_KV_LORA = D_MODEL // 8
MLA_NOPE = 64
MLA_ROPE = 32
MLA_VD = 64
MLA_WIDTH = MLA_HEADS * MLA_VD
MLA_SCALE = 1.0 / math.sqrt(MLA_NOPE + MLA_ROPE)
DIFF_HEADS = 4
DIFF_KV_HEADS = 2
DIFF_GROUP = DIFF_HEADS // DIFF_KV_HEADS
DIFF_HD = 64
DIFF_VD = 2 * DIFF_HD
DIFF_WIDTH = DIFF_HEADS * DIFF_VD
DIFF_SCALE = 1.0 / math.sqrt(DIFF_HD)
N_BRANCH = 3
D_FF = 4 * D_MODEL
ROPE_THETA = 10000.0
EPS = 1e-6
Q_BLOCK = 128
IN_SIZES = (SSM_WIDTH, MLA_Q_LORA, MLA_KV_LORA, MLA_ROPE,
            DIFF_HEADS * 2 * DIFF_HD, DIFF_KV_HEADS * 2 * DIFF_HD, DIFF_KV_HEADS * DIFF_VD,
            N_BRANCH * D_MODEL)
IN_COLS = sum(IN_SIZES)

kernel_name = 'hybrid_s5_mla_diffattn_decode_step'


def rmsnorm(x, g):
    xf = x.astype(jnp.float32)
    y = xf * lax.rsqrt(jnp.mean(xf * xf, axis=-1, keepdims=True) + EPS)
    return (y * g.astype(jnp.float32)).astype(x.dtype)


def rope(x, pos):
    d = x.shape[-1]
    half = d // 2
    inv = ROPE_THETA ** (-jnp.arange(half, dtype=jnp.float32) * 2.0 / d)
    ang = pos.astype(jnp.float32)[:, None] * inv[None, :]
    shape = (1, pos.shape[0]) + (1,) * (x.ndim - 3) + (half,)
    cos = jnp.cos(ang).reshape(shape)
    sin = jnp.sin(ang).reshape(shape)
    xf = x.astype(jnp.float32)
    x1 = xf[..., :half]
    x2 = xf[..., half:]
    return jnp.concatenate([x1 * cos - x2 * sin, x2 * cos + x1 * sin], axis=-1).astype(x.dtype)


def split_cols(z):
    outs = []
    start = 0
    for s in IN_SIZES:
        outs.append(z[..., start:start + s])
        start += s
    return outs


def attend(q, k, v, map_w, q_pos, k_pos, scale):
    f32 = jnp.float32
    q = q.astype(f32)
    k = k.astype(f32)
    v = v.astype(f32)
    w = map_w.astype(f32)

    def block(args):
        qb, qp = args
        s = jnp.einsum('bqhgmd,bkhmd->bhgmqk', qb, k) * scale
        s = jnp.where(k_pos[None, :] <= qp[:, None], s, -jnp.inf)
        p = jax.nn.softmax(s, axis=-1)
        a = jnp.einsum('bhgmqk,m->bhgqk', p, w)
        return jnp.einsum('bhgqk,bkhd->bqhgd', a, v)

    nb, nq = q.shape[0], q.shape[1]
    if nq <= Q_BLOCK:
        return block((q, q_pos))
    nblk = nq // Q_BLOCK
    qb = jnp.moveaxis(q.reshape((nb, nblk, Q_BLOCK) + q.shape[2:]), 1, 0)
    o = lax.map(block, (qb, q_pos.reshape(nblk, Q_BLOCK)))
    o = jnp.moveaxis(o, 0, 1)
    return o.reshape((nb, nq) + o.shape[3:])


def cplx_affine_combine(e1, e2):
    a1r, a1i, b1r, b1i = e1
    a2r, a2i, b2r, b2i = e2
    return (a1r * a2r - a1i * a2i,
            a1r * a2i + a1i * a2r,
            a2r * b1r - a2i * b1i + b2r,
            a2r * b1i + a2i * b1r + b2i)


def s5_branch(u, s0_re, s0_im, lp):
    nb, L, G, J = u.shape
    f32 = jnp.float32
    u = u.astype(f32)
    lr = lp['ssm_lambda_re'].astype(f32)
    li = lp['ssm_lambda_im'].astype(f32)
    step = jnp.exp(lp['ssm_log_dt'].astype(f32))[:, None]
    mag = jnp.exp(lr * step)
    ar = mag * jnp.cos(li * step)
    ai = mag * jnp.sin(li * step)
    den = lr * lr + li * li
    zr = ((ar - 1.0) * lr + ai * li) / den
    zi = (ai * lr - (ar - 1.0) * li) / den
    br = lp['ssm_b_re'].astype(f32)
    bi = lp['ssm_b_im'].astype(f32)
    bbr = zr[..., None] * br - zi[..., None] * bi
    bbi = zr[..., None] * bi + zi[..., None] * br
    xr = jnp.einsum('blgj,gnj->blgn', u, bbr)
    xi = jnp.einsum('blgj,gnj->blgn', u, bbi)
    s0r = s0_re.astype(f32)
    s0i = s0_im.astype(f32)
    xr = xr.at[:, 0].add(ar * s0r - ai * s0i)
    xi = xi.at[:, 0].add(ar * s0i + ai * s0r)
    a_r = jnp.broadcast_to(ar, xr.shape)
    a_i = jnp.broadcast_to(ai, xi.shape)
    _, _, sr, si = lax.associative_scan(cplx_affine_combine, (a_r, a_i, xr, xi), axis=1)
    y = (jnp.einsum('blgn,gjn->blgj', sr, lp['ssm_c_re'].astype(f32))
         - jnp.einsum('blgn,gjn->blgj', si, lp['ssm_c_im'].astype(f32))
         + lp['ssm_d'].astype(f32) * u)
    y = jax.nn.gelu(y.reshape(nb, L, G * J))
    y = y * jax.nn.sigmoid(y @ lp['ssm_w_glu'].astype(f32) + lp['ssm_b_glu'].astype(f32))
    return y, sr[:, -1].astype(s0_re.dtype), si[:, -1].astype(s0_im.dtype)


def mix_block(h, pos, s0_re, s0_im, past, lp, layer_idx):
    nb, L, _ = h.shape
    dt = h.dtype
    f32 = jnp.float32
    z = h @ lp['w_in']
    u, cq, ckv, kr, dq, dk, dv, gz = split_cols(z)

    y_ssm, s_re, s_im = s5_branch(u.reshape(nb, L, SSM_GROUPS, SSM_GROUP), s0_re, s0_im, lp)

    cq = rmsnorm(cq, lp['mla_g_q'])
    q = (cq @ lp['mla_w_uq']).reshape(nb, L, MLA_HEADS, MLA_NOPE + MLA_ROPE)
    q_nope = q[..., :MLA_NOPE]
    q_rope = rope(q[..., MLA_NOPE:], pos)
    ckv = rmsnorm(ckv, lp['mla_g_kv'])
    kr = rope(kr, pos)
    q_lat = jnp.einsum('blhn,chn->blhc', q_nope.astype(f32), lp['mla_w_uk'].astype(f32))
    q_mla = jnp.concatenate([q_lat, q_rope.astype(f32)], axis=-1)[:, :, None, :, None, :]
    k_mla = jnp.concatenate([ckv, kr], axis=-1)
    v_mla = ckv
    kd = rope(dk.reshape(nb, L, DIFF_KV_HEADS, 2, DIFF_HD), pos)
    vd = dv.reshape(nb, L, DIFF_KV_HEADS, DIFF_VD)
    if past is None:
        k_mla_all, v_mla_all, kd_all, vd_all = k_mla, v_mla, kd, vd
    else:
        lat_p, kr_p, dk_p, dv_p = past
        k_mla_all = jnp.concatenate([jnp.concatenate([lat_p, kr_p], axis=-1), k_mla], axis=1)
        v_mla_all = jnp.concatenate([lat_p, v_mla], axis=1)
        kd_all = jnp.concatenate([dk_p, kd], axis=1)
        vd_all = jnp.concatenate([dv_p, vd], axis=1)
    k_pos = jnp.arange(k_mla_all.shape[1])
    o = attend(q_mla, k_mla_all[:, :, None, None, :], v_mla_all[:, :, None, :],
               jnp.ones((1,), f32), pos, k_pos, MLA_SCALE)
    o = o.reshape(nb, L, MLA_HEADS, MLA_KV_LORA)
    y_mla = jnp.einsum('blhc,chd->blhd', o, lp['mla_w_uv'].astype(f32)).reshape(nb, L, MLA_WIDTH)

    qd = rope(dq.reshape(nb, L, DIFF_KV_HEADS, DIFF_GROUP, 2, DIFF_HD), pos)
    lam_init = 0.8 - 0.6 * math.exp(-0.3 * layer_idx)
    lam = (jnp.exp(jnp.sum(lp['diff_lambda_q1'].astype(f32) * lp['diff_lambda_k1'].astype(f32)))
           - jnp.exp(jnp.sum(lp['diff_lambda_q2'].astype(f32) * lp['diff_lambda_k2'].astype(f32)))
           + lam_init)
    od = attend(qd, kd_all, vd_all, jnp.stack([jnp.ones((), f32), -lam]), pos, k_pos, DIFF_SCALE)
    od = rmsnorm(od.reshape(nb, L, DIFF_HEADS, DIFF_VD), lp['diff_g_sub']) * (1.0 - lam_init)
    y_diff = od.reshape(nb, L, DIFF_WIDTH)

    gates = jax.nn.sigmoid((gz.reshape(nb, L, N_BRANCH, D_MODEL) + lp['b_gate']).astype(f32))
    merged = (gates[:, :, 0] * (y_ssm.astype(dt) @ lp['w_br_ssm']).astype(f32)
              + gates[:, :, 1] * (y_mla.astype(dt) @ lp['w_br_mla']).astype(f32)
              + gates[:, :, 2] * (y_diff.astype(dt) @ lp['w_br_diff']).astype(f32))
    out = merged.astype(dt) @ lp['w_out']
    return out, (ckv, kr, kd, vd, s_re, s_im)


def run_trunk(x, pos, s0_re, s0_im, caches, page_table, P):
    nb = x.shape[0]
    rows = ([], [], [], [], [], [])
    for l in range(DEPTH):
        lp = {name: arr[l] for name, arr in P.items()}
        past = None
        if caches is not None:
            past = tuple(c[l, page_table].reshape((nb, -1) + c.shape[3:]) for c in caches)
        h = rmsnorm(x, lp['g_mix'])
        mix, new_rows = mix_block(h, pos, s0_re[l], s0_im[l], past, lp, l)
        x = x + mix
        h = rmsnorm(x, lp['g_mlp'])
        x = x + jnp.square(jax.nn.relu(h @ lp['w_up'])) @ lp['w_down']
        for acc, r in zip(rows, new_rows):
            acc.append(r)
    return x, [jnp.stack(acc) for acc in rows]


def setup_inputs(seed: int = 0) -> dict:
    key = jax.random.key(seed)
    ks = iter(jax.random.split(key, 48))
    f32 = jnp.float32

    def nrm(shape, scale):
        return scale * jax.random.normal(next(ks), shape, f32)

    def gain(shape):
        return 1.0 + nrm(shape, 0.02)

    n_pages = PAST_LEN // PAGE_SIZE
    n_used = DEC_BATCH * n_pages
    n_pool = n_used + n_used // 4
    x_prompt = nrm((BATCH, SEQ, D_MODEL), 1.0)
    x_sample = nrm((DEC_BATCH, DEC_SEQ, D_MODEL), 1.0)
    cache_mla_latent = nrm((DEPTH, n_pool, PAGE_SIZE, MLA_KV_LORA), 1.0)
    cache_mla_krope = nrm((DEPTH, n_pool, PAGE_SIZE, MLA_ROPE), 1.0)
    cache_diff_k = nrm((DEPTH, n_pool, PAGE_SIZE, DIFF_KV_HEADS, 2, DIFF_HD), 1.0)
    cache_diff_v = nrm((DEPTH, n_pool, PAGE_SIZE, DIFF_KV_HEADS, DIFF_VD), 1.0)
    state_ssm_re = nrm((DEPTH, DEC_BATCH, SSM_GROUPS, SSM_STATE), 0.5)
    state_ssm_im = nrm((DEPTH, DEC_BATCH, SSM_GROUPS, SSM_STATE), 0.5)
    page_table = jax.random.permutation(next(ks), n_pool)[:n_used].reshape(DEC_BATCH, n_pages).astype(jnp.int32)

    g_mix = gain((DEPTH, D_MODEL))
    w_in = nrm((DEPTH, D_MODEL, IN_COLS), D_MODEL ** -0.5)
    b_gate = nrm((DEPTH, N_BRANCH, D_MODEL), 0.01)
    ssm_lambda_re = -0.5 + nrm((DEPTH, SSM_GROUPS, SSM_STATE), 0.01)
    ssm_lambda_im = (jnp.broadcast_to(math.pi * jnp.arange(SSM_STATE, dtype=f32), (DEPTH, SSM_GROUPS, SSM_STATE))
                     + nrm((DEPTH, SSM_GROUPS, SSM_STATE), 0.01))
    ssm_log_dt = jax.random.uniform(next(ks), (DEPTH, SSM_GROUPS), f32,
                                    minval=math.log(SSM_DT_MIN), maxval=math.log(SSM_DT_MAX))
    ssm_b_re = nrm((DEPTH, SSM_GROUPS, SSM_STATE, SSM_GROUP), (2 * SSM_GROUP) ** -0.5)
    ssm_b_im = nrm((DEPTH, SSM_GROUPS, SSM_STATE, SSM_GROUP), (2 * SSM_GROUP) ** -0.5)
    ssm_c_re = nrm((DEPTH, SSM_GROUPS, SSM_GROUP, SSM_STATE), SSM_STATE ** -0.5)
    ssm_c_im = nrm((DEPTH, SSM_GROUPS, SSM_GROUP, SSM_STATE), SSM_STATE ** -0.5)
    ssm_d = nrm((DEPTH, SSM_GROUPS, SSM_GROUP), 1.0)
    ssm_w_glu = nrm((DEPTH, SSM_WIDTH, SSM_WIDTH), SSM_WIDTH ** -0.5)
    ssm_b_glu = nrm((DEPTH, SSM_WIDTH), 0.01)
    mla_g_q = gain((DEPTH, MLA_Q_LORA))
    mla_w_uq = nrm((DEPTH, MLA_Q_LORA, MLA_HEADS * (MLA_NOPE + MLA_ROPE)), MLA_Q_LORA ** -0.5)
    mla_g_kv = gain((DEPTH, MLA_KV_LORA))
    mla_w_uk = nrm((DEPTH, MLA_KV_LORA, MLA_HEADS, MLA_NOPE), MLA_KV_LORA ** -0.5)
    mla_w_uv = nrm((DEPTH, MLA_KV_LORA, MLA_HEADS, MLA_VD), MLA_KV_LORA ** -0.5)
    diff_lambda_q1 = nrm((DEPTH, DIFF_HD), 0.1)
    diff_lambda_k1 = nrm((DEPTH, DIFF_HD), 0.1)
    diff_lambda_q2 = nrm((DEPTH, DIFF_HD), 0.1)
    diff_lambda_k2 = nrm((DEPTH, DIFF_HD), 0.1)
    diff_g_sub = gain((DEPTH, DIFF_VD))
    w_br_ssm = nrm((DEPTH, SSM_WIDTH, D_MODEL), SSM_WIDTH ** -0.5)
    w_br_mla = nrm((DEPTH, MLA_WIDTH, D_MODEL), MLA_WIDTH ** -0.5)
    w_br_diff = nrm((DEPTH, DIFF_WIDTH, D_MODEL), DIFF_WIDTH ** -0.5)
    w_out = nrm((DEPTH, D_MODEL, D_MODEL), D_MODEL ** -0.5)
    g_mlp = gain((DEPTH, D_MODEL))
    w_up = nrm((DEPTH, D_MODEL, D_FF), D_MODEL ** -0.5)
    w_down = nrm((DEPTH, D_FF, D_MODEL), D_FF ** -0.5)
    g_final = gain((D_MODEL,))
    return {
        'x_prompt': x_prompt, 'x_sample': x_sample,
        'cache_mla_latent': cache_mla_latent, 'cache_mla_krope': cache_mla_krope,
        'cache_diff_k': cache_diff_k, 'cache_diff_v': cache_diff_v,
        'state_ssm_re': state_ssm_re, 'state_ssm_im': state_ssm_im,
        'page_table': page_table,
        'g_mix': g_mix, 'w_in': w_in, 'b_gate': b_gate,
        'ssm_lambda_re': ssm_lambda_re, 'ssm_lambda_im': ssm_lambda_im, 'ssm_log_dt': ssm_log_dt,
        'ssm_b_re': ssm_b_re, 'ssm_b_im': ssm_b_im, 'ssm_c_re': ssm_c_re, 'ssm_c_im': ssm_c_im,
        'ssm_d': ssm_d, 'ssm_w_glu': ssm_w_glu, 'ssm_b_glu': ssm_b_glu,
        'mla_g_q': mla_g_q, 'mla_w_uq': mla_w_uq, 'mla_g_kv': mla_g_kv,
        'mla_w_uk': mla_w_uk, 'mla_w_uv': mla_w_uv,
        'diff_lambda_q1': diff_lambda_q1, 'diff_lambda_k1': diff_lambda_k1,
        'diff_lambda_q2': diff_lambda_q2, 'diff_lambda_k2': diff_lambda_k2, 'diff_g_sub': diff_g_sub,
        'w_br_ssm': w_br_ssm, 'w_br_mla': w_br_mla, 'w_br_diff': w_br_diff, 'w_out': w_out,
        'g_mlp': g_mlp, 'w_up': w_up, 'w_down': w_down, 'g_final': g_final,
    }


def reference(x_prompt, x_sample, cache_mla_latent, cache_mla_krope, cache_diff_k, cache_diff_v,
              state_ssm_re, state_ssm_im, page_table,
              g_mix, w_in, b_gate,
              ssm_lambda_re, ssm_lambda_im, ssm_log_dt, ssm_b_re, ssm_b_im, ssm_c_re, ssm_c_im,
              ssm_d, ssm_w_glu, ssm_b_glu,
              mla_g_q, mla_w_uq, mla_g_kv, mla_w_uk, mla_w_uv,
              diff_lambda_q1, diff_lambda_k1, diff_lambda_q2, diff_lambda_k2, diff_g_sub,
              w_br_ssm, w_br_mla, w_br_diff, w_out,
              g_mlp, w_up, w_down, g_final):
    P = dict(g_mix=g_mix, w_in=w_in, b_gate=b_gate,
             ssm_lambda_re=ssm_lambda_re, ssm_lambda_im=ssm_lambda_im, ssm_log_dt=ssm_log_dt,
             ssm_b_re=ssm_b_re, ssm_b_im=ssm_b_im, ssm_c_re=ssm_c_re, ssm_c_im=ssm_c_im,
             ssm_d=ssm_d, ssm_w_glu=ssm_w_glu, ssm_b_glu=ssm_b_glu,
             mla_g_q=mla_g_q, mla_w_uq=mla_w_uq, mla_g_kv=mla_g_kv, mla_w_uk=mla_w_uk, mla_w_uv=mla_w_uv,
             diff_lambda_q1=diff_lambda_q1, diff_lambda_k1=diff_lambda_k1,
             diff_lambda_q2=diff_lambda_q2, diff_lambda_k2=diff_lambda_k2, diff_g_sub=diff_g_sub,
             w_br_ssm=w_br_ssm, w_br_mla=w_br_mla, w_br_diff=w_br_diff, w_out=w_out,
             g_mlp=g_mlp, w_up=w_up, w_down=w_down)
    nb_p, seq_p = x_prompt.shape[0], x_prompt.shape[1]
    zero_state = jnp.zeros((DEPTH, nb_p, SSM_GROUPS, SSM_STATE), state_ssm_re.dtype)
    hp, p_rows = run_trunk(x_prompt, jnp.arange(seq_p), zero_state, zero_state, None, None, P)
    past_len = page_table.shape[1] * PAGE_SIZE
    pos_s = past_len + jnp.arange(x_sample.shape[1])
    hs, s_rows = run_trunk(x_sample, pos_s, state_ssm_re, state_ssm_im,
                           (cache_mla_latent, cache_mla_krope, cache_diff_k, cache_diff_v), page_table, P)
    y_prompt = rmsnorm(hp, g_final)
    y_sample = rmsnorm(hs, g_final)
    p_lat, p_kr, p_dk, p_dv, p_sre, p_sim = p_rows
    s_lat, s_kr, s_dk, s_dv, s_sre, s_sim = s_rows
    return (y_prompt, y_sample, p_lat, p_kr, p_dk, p_dv, p_sre, p_sim, s_lat, s_kr, s_dk, s_dv, s_sre, s_sim)
```

```python
import functools
import math

import jax
import jax.numpy as jnp
from jax import lax
from jax.experimental import pallas as pl
from jax.experimental.pallas import tpu as pltpu

F32 = jnp.float32
BF16 = jnp.bfloat16

SSM_GROUP = 16
SSM_STATE = 64
MLA_HEADS = 8
MLA_NOPE = 64
MLA_ROPE = 32
DIFF_HEADS = 4
DIFF_KV_HEADS = 2
DIFF_GROUP = DIFF_HEADS // DIFF_KV_HEADS
DIFF_HD = 64
DIFF_VD = 2 * DIFF_HD
ROPE_THETA = 10000.0
EPS = 1e-6

LANE = 128
SUBLANE = 8
VMEM_LIMIT = 56 * 1024 * 1024


def _cparams(*sem):
    return pltpu.CompilerParams(dimension_semantics=sem, vmem_limit_bytes=VMEM_LIMIT)


def _rms(x, g):
    ms = jnp.mean(x * x, axis=-1, keepdims=True)
    return x * lax.rsqrt(ms + EPS) * g


def _mm_kernel(x_ref, w_ref, o_ref):
    o_ref[...] = jnp.dot(x_ref[...].astype(BF16), w_ref[...],
                         preferred_element_type=F32).astype(o_ref.dtype)


def _mm(x, w, tm, out_dtype=F32):
    t, k = x.shape
    n = w.shape[1]
    return pl.pallas_call(
        _mm_kernel,
        grid=(t // tm,),
        in_specs=[pl.BlockSpec((tm, k), lambda i: (i, 0)),
                  pl.BlockSpec((k, n), lambda i: (0, 0))],
        out_specs=pl.BlockSpec((tm, n), lambda i: (i, 0)),
        out_shape=jax.ShapeDtypeStruct((t, n), out_dtype),
        compiler_params=_cparams("parallel"),
        name="mm",
    )(x, w)


def _norm_mm_kernel(x_ref, g_ref, w_ref, o_ref):
    h = _rms(x_ref[...], g_ref[...]).astype(BF16)
    o_ref[...] = jnp.dot(h, w_ref[...], preferred_element_type=F32)


def _norm_mm(x, g, w, tm, n_col_blocks):
    t, k = x.shape
    n = w.shape[1]
    tn = n // n_col_blocks
    return pl.pallas_call(
        _norm_mm_kernel,
        grid=(n_col_blocks, t // tm),
        in_specs=[pl.BlockSpec((tm, k), lambda j, i: (i, 0)),
                  pl.BlockSpec((1, k), lambda j, i: (0, 0)),
                  pl.BlockSpec((k, tn), lambda j, i: (0, j))],
        out_specs=pl.BlockSpec((tm, tn), lambda j, i: (i, j)),
        out_shape=jax.ShapeDtypeStruct((t, n), F32),
        compiler_params=_cparams("parallel", "parallel"),
        name="norm_mm",
    )(x, g, w)


def _merge_kernel(x_ref, ys_ref, ym_ref, yd_ref, gz_ref, bg_ref, ws_ref, wm_ref, wd_ref, wo_ref, o_ref):
    d = x_ref.shape[1]
    merged = None
    for b, (y_ref, w_ref) in enumerate(((ys_ref, ws_ref), (ym_ref, wm_ref), (yd_ref, wd_ref))):
        gate = jax.nn.sigmoid(gz_ref[:, b * d:(b + 1) * d] + bg_ref[:, b * d:(b + 1) * d])
        term = gate * jnp.dot(y_ref[...].astype(BF16), w_ref[...], preferred_element_type=F32)
        merged = term if merged is None else merged + term
    o_ref[...] = x_ref[...] + jnp.dot(merged.astype(BF16), wo_ref[...], preferred_element_type=F32)


def _merge(x, y_ssm, y_mla, y_diff, gz, b_gate, w_s, w_m, w_d, w_o, tm):
    t, d = x.shape
    row = lambda i: (i, 0)
    const = lambda i: (0, 0)
    return pl.pallas_call(
        _merge_kernel,
        grid=(t // tm,),
        in_specs=[pl.BlockSpec((tm, d), row),
                  pl.BlockSpec((tm, y_ssm.shape[1]), row),
                  pl.BlockSpec((tm, y_mla.shape[1]), row),
                  pl.BlockSpec((tm, y_diff.shape[1]), row),
                  pl.BlockSpec((tm, gz.shape[1]), row),
                  pl.BlockSpec(b_gate.shape, const),
                  pl.BlockSpec(w_s.shape, const),
                  pl.BlockSpec(w_m.shape, const),
                  pl.BlockSpec(w_d.shape, const),
                  pl.BlockSpec(w_o.shape, const)],
        out_specs=pl.BlockSpec((tm, d), row),
        out_shape=jax.ShapeDtypeStruct((t, d), F32),
        compiler_params=_cparams("parallel"),
        name="merge",
    )(x, y_ssm, y_mla, y_diff, gz, b_gate, w_s, w_m, w_d, w_o)


def _mlp_kernel(x_ref, g_ref, wu_ref, wd_ref, o_ref, h_sc):
    f = pl.program_id(1)

    @pl.when(f == 0)
    def _():
        x = x_ref[...]
        h_sc[...] = _rms(x, g_ref[...]).astype(BF16)
        o_ref[...] = x

    a = jnp.dot(h_sc[...], wu_ref[...], preferred_element_type=F32)
    a = jnp.square(jnp.maximum(a, 0.0)).astype(BF16)
    o_ref[...] += jnp.dot(a, wd_ref[...], preferred_element_type=F32)


def _mlp(x, g, w_up, w_down, tm, tf):
    t, d = x.shape
    ff = w_up.shape[1]
    return pl.pallas_call(
        _mlp_kernel,
        grid=(t // tm, ff // tf),
        in_specs=[pl.BlockSpec((tm, d), lambda i, f: (i, 0)),
                  pl.BlockSpec((1, d), lambda i, f: (0, 0)),
                  pl.BlockSpec((d, tf), lambda i, f: (0, f)),
                  pl.BlockSpec((tf, d), lambda i, f: (f, 0))],
        out_specs=pl.BlockSpec((tm, d), lambda i, f: (i, 0)),
        out_shape=jax.ShapeDtypeStruct((t, d), F32),
        scratch_shapes=[pltpu.VMEM((tm, d), BF16)],
        compiler_params=_cparams("parallel", "arbitrary"),
        name="mlp",
    )(x, g, w_up, w_down)


def _norm_kernel(x_ref, g_ref, o_ref):
    o_ref[...] = _rms(x_ref[...], g_ref[...])


def _norm(x, g, tm):
    t, d = x.shape
    return pl.pallas_call(
        _norm_kernel,
        grid=(t // tm,),
        in_specs=[pl.BlockSpec((tm, d), lambda i: (i, 0)), pl.BlockSpec((1, d), lambda i: (0, 0))],
        out_specs=pl.BlockSpec((tm, d), lambda i: (i, 0)),
        out_shape=jax.ShapeDtypeStruct((t, d), F32),
        compiler_params=_cparams("parallel"),
        name="final_norm",
    )(x, g)


SCAN_COLS = 4 * LANE


def _cplx_step(ar, ai, sr, si, xr, xi):
    return ar * sr - ai * si + xr, ar * si + ai * sr + xi


def _ssm_prompt_kernel(u_ref, wb_ref, a_ref, ap_ref, wc_ref, d_ref, y_ref, fin_ref,
                       xs, state_sc, carry_sc, fin_sc, *, chunk_len, ns):
    li = pl.program_id(1)

    @pl.when(li == 0)
    def _():
        state_sc[...] = jnp.zeros_like(state_sc)

    u = u_ref[0]
    xs[...] = jnp.dot(u.astype(BF16), wb_ref[...], preferred_element_type=F32)

    n_cb = ns // SCAN_COLS
    for cb in range(n_cb):
        re = pl.ds(cb * SCAN_COLS, SCAN_COLS)
        im = pl.ds(ns + cb * SCAN_COLS, SCAN_COLS)
        ar = jnp.broadcast_to(a_ref[0:1, re], (SUBLANE, SCAN_COLS))
        ai = jnp.broadcast_to(a_ref[0:1, im], (SUBLANE, SCAN_COLS))

        def sweep1(t, s):
            rows = pl.ds(pl.multiple_of(t * SUBLANE, SUBLANE), SUBLANE)
            return _cplx_step(ar, ai, s[0], s[1], xs[rows, re], xs[rows, im])

        zero = jnp.zeros((SUBLANE, SCAN_COLS), F32)
        fr, fi = lax.fori_loop(0, chunk_len, sweep1, (zero, zero))
        fin_sc[:, re] = fr
        fin_sc[:, im] = fi

    carry_sc[0:1, :] = state_sc[0:1, :]
    pr = ap_ref[0:1, 0:ns]
    pi = ap_ref[0:1, ns:2 * ns]
    for c in range(1, SUBLANE):
        cr = carry_sc[c - 1:c, 0:ns]
        ci = carry_sc[c - 1:c, ns:2 * ns]
        nr, ni = _cplx_step(pr, pi, cr, ci, fin_sc[c - 1:c, 0:ns], fin_sc[c - 1:c, ns:2 * ns])
        carry_sc[c:c + 1, 0:ns] = nr
        carry_sc[c:c + 1, ns:2 * ns] = ni

    for cb in range(n_cb):
        re = pl.ds(cb * SCAN_COLS, SCAN_COLS)
        im = pl.ds(ns + cb * SCAN_COLS, SCAN_COLS)
        ar = jnp.broadcast_to(a_ref[0:1, re], (SUBLANE, SCAN_COLS))
        ai = jnp.broadcast_to(a_ref[0:1, im], (SUBLANE, SCAN_COLS))

        def sweep2(t, s):
            rows = pl.ds(pl.multiple_of(t * SUBLANE, SUBLANE), SUBLANE)
            nr, ni = _cplx_step(ar, ai, s[0], s[1], xs[rows, re], xs[rows, im])
            xs[rows, re] = nr
            xs[rows, im] = ni
            return nr, ni

        fr, fi = lax.fori_loop(0, chunk_len, sweep2, (carry_sc[:, re], carry_sc[:, im]))
        state_sc[0:1, re] = fr[SUBLANE - 1:SUBLANE]
        state_sc[0:1, im] = fi[SUBLANE - 1:SUBLANE]

    y_ref[0] = jnp.dot(xs[...].astype(BF16), wc_ref[...], preferred_element_type=F32) + d_ref[...] * u
    fin_ref[0] = state_sc[0:1, :]


def _ssm_prompt(u_perm, wb, a_vec, a_pow, wc, d_vec, tl):
    nb, seq, w = u_perm.shape
    ns2 = wb.shape[1]
    const = lambda b, i: (0, 0)
    kern = functools.partial(_ssm_prompt_kernel, chunk_len=tl // SUBLANE, ns=ns2 // 2)
    return pl.pallas_call(
        kern,
        grid=(nb, seq // tl),
        in_specs=[pl.BlockSpec((1, tl, w), lambda b, i: (b, i, 0)),
                  pl.BlockSpec(wb.shape, const),
                  pl.BlockSpec(a_vec.shape, const),
                  pl.BlockSpec(a_pow.shape, const),
                  pl.BlockSpec(wc.shape, const),
                  pl.BlockSpec(d_vec.shape, const)],
        out_specs=[pl.BlockSpec((1, tl, w), lambda b, i: (b, i, 0)),
                   pl.BlockSpec((1, 1, ns2), lambda b, i: (b, 0, 0))],
        out_shape=[jax.ShapeDtypeStruct((nb, seq, w), F32),
                   jax.ShapeDtypeStruct((nb, 1, ns2), F32)],
        scratch_shapes=[pltpu.VMEM((tl, ns2), F32),
                        pltpu.VMEM((SUBLANE, ns2), F32),
                        pltpu.VMEM((SUBLANE, ns2), F32),
                        pltpu.VMEM((SUBLANE, ns2), F32)],
        compiler_params=_cparams("parallel", "arbitrary"),
        name="ssm_prompt",
    )(u_perm, wb, a_vec, a_pow, wc, d_vec)


def _ssm_sample_kernel(u_ref, s0_ref, wb_ref, a_ref, wc_ref, d_ref, y_ref, fin_ref, xs, *, steps, ns):
    u = u_ref[0]
    xs[...] = jnp.dot(u.astype(BF16), wb_ref[...], preferred_element_type=F32)
    for cb in range(ns // SCAN_COLS):
        re = pl.ds(cb * SCAN_COLS, SCAN_COLS)
        im = pl.ds(ns + cb * SCAN_COLS, SCAN_COLS)
        ar = jnp.broadcast_to(a_ref[0:1, re], (SUBLANE, SCAN_COLS))
        ai = jnp.broadcast_to(a_ref[0:1, im], (SUBLANE, SCAN_COLS))
        sr = s0_ref[0, :, re]
        si = s0_ref[0, :, im]
        for t in range(steps):
            rows = pl.ds(t * SUBLANE, SUBLANE)
            sr, si = _cplx_step(ar, ai, sr, si, xs[rows, re], xs[rows, im])
            xs[rows, re] = sr
            xs[rows, im] = si
        fin_ref[0, :, re] = sr
        fin_ref[0, :, im] = si
    y_ref[0] = jnp.dot(xs[...].astype(BF16), wc_ref[...], preferred_element_type=F32) + d_ref[...] * u


def _ssm_sample(u_grp, s0_grp, wb, a_vec, wc, d_vec, steps):
    ng, rows, w = u_grp.shape
    ns2 = wb.shape[1]
    const = lambda g: (0, 0)
    kern = functools.partial(_ssm_sample_kernel, steps=steps, ns=ns2 // 2)
    return pl.pallas_call(
        kern,
        grid=(ng,),
        in_specs=[pl.BlockSpec((1, rows, w), lambda g: (g, 0, 0)),
                  pl.BlockSpec((1, SUBLANE, ns2), lambda g: (g, 0, 0)),
                  pl.BlockSpec(wb.shape, const),
                  pl.BlockSpec(a_vec.shape, const),
                  pl.BlockSpec(wc.shape, const),
                  pl.BlockSpec(d_vec.shape, const)],
        out_specs=[pl.BlockSpec((1, rows, w), lambda g: (g, 0, 0)),
                   pl.BlockSpec((1, SUBLANE, ns2), lambda g: (g, 0, 0))],
        out_shape=[jax.ShapeDtypeStruct((ng, rows, w), F32),
                   jax.ShapeDtypeStruct((ng, SUBLANE, ns2), F32)],
        scratch_shapes=[pltpu.VMEM((rows, ns2), F32)],
        compiler_params=_cparams("parallel"),
        name="ssm_sample",
    )(u_grp, s0_grp, wb, a_vec, wc, d_vec)


def _flash_kernel(q_ref, kt_ref, v_ref, o_ref, m_sc, l_sc, acc_sc, *, scale):
    _, h, tq, d = q_ref.shape
    tk = kt_ref.shape[3]
    i = pl.program_id(1)
    q = q_ref[0].reshape(h * tq, d)
    m_sc[...] = jnp.full_like(m_sc, -jnp.inf)
    l_sc[...] = jnp.zeros_like(l_sc)
    acc_sc[...] = jnp.zeros_like(acc_sc)

    def chunk(j, masked):
        s = jnp.dot(q, kt_ref[0, j], preferred_element_type=F32)
        if scale != 1.0:
            s = s * scale
        if masked:
            s3 = s.reshape(h, tq, tk)
            row = lax.broadcasted_iota(jnp.int32, (tq, tk), 0)
            col = lax.broadcasted_iota(jnp.int32, (tq, tk), 1)
            s = jnp.where((col <= row)[None], s3, -jnp.inf).reshape(h * tq, tk)
        m_prev = m_sc[...]
        m_new = jnp.maximum(m_prev, jnp.max(s, axis=-1, keepdims=True))
        alpha = jnp.exp(m_prev - m_new)
        p = jnp.exp(s - m_new)
        l_sc[...] = alpha * l_sc[...] + jnp.sum(p, axis=-1, keepdims=True)
        acc_sc[...] = alpha * acc_sc[...] + jnp.dot(p.astype(BF16), v_ref[0, j], preferred_element_type=F32)
        m_sc[...] = m_new

    def body(j, carry):
        chunk(j, False)
        return carry

    lax.fori_loop(0, i, body, 0)
    chunk(i, True)
    o_ref[0] = (acc_sc[...] / l_sc[...]).reshape(h, tq, acc_sc.shape[1])


def _flash(q, kt, v, tile, scale):
    nb, h, seq, d = q.shape
    dv = v.shape[3]
    kern = functools.partial(_flash_kernel, scale=scale)
    return pl.pallas_call(
        kern,
        grid=(nb, seq // tile),
        in_specs=[pl.BlockSpec((1, h, tile, d), lambda b, i: (b, 0, i, 0)),
                  pl.BlockSpec((1,) + kt.shape[1:], lambda b, i: (b, 0, 0, 0)),
                  pl.BlockSpec((1,) + v.shape[1:], lambda b, i: (b, 0, 0, 0))],
        out_specs=pl.BlockSpec((1, h, tile, dv), lambda b, i: (b, 0, i, 0)),
        out_shape=jax.ShapeDtypeStruct((nb, h, seq, dv), F32),
        scratch_shapes=[pltpu.VMEM((h * tile, 1), F32),
                        pltpu.VMEM((h * tile, 1), F32),
                        pltpu.VMEM((h * tile, dv), F32)],
        compiler_params=_cparams("parallel", "parallel"),
        name="flash",
    )(q, kt, v)


PAGES_PER_STEP = 8


def _online_update(s, v, m_sc, l_sc, acc_sc):
    m_prev = m_sc[...]
    m_new = jnp.maximum(m_prev, jnp.max(s, axis=-1, keepdims=True))
    alpha = jnp.exp(m_prev - m_new)
    p = jnp.exp(s - m_new)
    l_sc[...] = alpha * l_sc[...] + jnp.sum(p, axis=-1, keepdims=True)
    acc_sc[...] = alpha * acc_sc[...] + jnp.dot(p.astype(BF16), v, preferred_element_type=F32)
    m_sc[...] = m_new


def _nt_dot(a, b):
    return lax.dot_general(a, b, (((1,), (1,)), ((), ())), preferred_element_type=F32)


def _paged_kernel(pt_ref, qm_ref, qd_ref, nlat_ref, nkr_ref, ndk_ref, ndv_ref, *rest,
                  n_pages, steps, mla_scale, lat_w, rope_w):
    pages = rest[:4 * n_pages]
    om_ref, od_ref, mm_sc, lm_sc, am_sc, md_sc, ld_sc, ad_sc = rest[4 * n_pages:]
    lat_refs = pages[0:n_pages]
    kr_refs = pages[n_pages:2 * n_pages]
    dk_refs = pages[2 * n_pages:3 * n_pages]
    dv_refs = pages[3 * n_pages:4 * n_pages]
    g = pl.program_id(1)
    n_kv = qd_ref.shape[1]
    hd2 = qd_ref.shape[3]
    vd = ad_sc.shape[2]

    @pl.when(g == 0)
    def _():
        mm_sc[...] = jnp.full_like(mm_sc, -jnp.inf)
        lm_sc[...] = jnp.zeros_like(lm_sc)
        am_sc[...] = jnp.zeros_like(am_sc)
        md_sc[...] = jnp.full_like(md_sc, -jnp.inf)
        ld_sc[...] = jnp.zeros_like(ld_sc)
        ad_sc[...] = jnp.zeros_like(ad_sc)

    q_lat = qm_ref[0, :, 0:lat_w]
    q_rope = qm_ref[0, :, lat_w:lat_w + rope_w]

    def attend_all(lat, kr, dk, dv, mask_fn):
        s = (_nt_dot(q_lat, lat) + _nt_dot(q_rope, kr)) * mla_scale
        if mask_fn is not None:
            s = mask_fn(s)
        _online_update(s, lat, mm_sc, lm_sc, am_sc)
        for kv in range(n_kv):
            sd = _nt_dot(qd_ref[0, kv], dk[:, kv * hd2:(kv + 1) * hd2])
            if mask_fn is not None:
                sd = mask_fn(sd)
            _online_update(sd, dv[:, kv * vd:(kv + 1) * vd], md_sc.at[kv], ld_sc.at[kv], ad_sc.at[kv])

    cat = lambda refs: jnp.concatenate([r[...].astype(BF16) for r in refs], axis=0)
    attend_all(cat(lat_refs), cat(kr_refs), cat(dk_refs), cat(dv_refs), None)

    @pl.when(g == pl.num_programs(1) - 1)
    def _():
        def mask_fn(s):
            row = lax.broadcasted_iota(jnp.int32, s.shape, 0) % steps
            col = lax.broadcasted_iota(jnp.int32, s.shape, 1)
            return jnp.where(col <= row, s, -jnp.inf)

        attend_all(nlat_ref[0].astype(BF16), nkr_ref[0].astype(BF16),
                   ndk_ref[0].astype(BF16), ndv_ref[0].astype(BF16), mask_fn)
        om_ref[0] = am_sc[...] / lm_sc[...]
        od_ref[0] = ad_sc[...] / ld_sc[...]


def _paged(page_table, layer, q_mla, q_diff, new_lat, new_kr, new_dk, new_dv,
           cache_lat, cache_kr, cache_dk, cache_dv, steps, mla_scale):
    nb, n_pages_total = page_table.shape
    npg = PAGES_PER_STEP
    page = cache_lat.shape[2]
    lat_w = cache_lat.shape[3]
    rope_w = cache_kr.shape[3]
    dk_w = cache_dk.shape[3]
    dv_w = cache_dv.shape[3]
    n_kv = q_diff.shape[1]
    rows_m = q_mla.shape[1]
    rows_d = q_diff.shape[2]

    def page_spec(width, p):
        return pl.BlockSpec((None, None, page, width),
                            lambda b, g, pt: (layer, pt[b, g * npg + p], 0, 0))

    per_b3 = lambda b, g, pt: (b, 0, 0)
    per_b4 = lambda b, g, pt: (b, 0, 0, 0)
    in_specs = [pl.BlockSpec((1,) + q_mla.shape[1:], per_b3),
                pl.BlockSpec((1,) + q_diff.shape[1:], per_b4),
                pl.BlockSpec((1,) + new_lat.shape[1:], per_b3),
                pl.BlockSpec((1,) + new_kr.shape[1:], per_b3),
                pl.BlockSpec((1,) + new_dk.shape[1:], per_b3),
                pl.BlockSpec((1,) + new_dv.shape[1:], per_b3)]
    operands = [q_mla, q_diff, new_lat, new_kr, new_dk, new_dv]
    for width, cache in ((lat_w, cache_lat), (rope_w, cache_kr), (dk_w, cache_dk), (dv_w, cache_dv)):
        for p in range(npg):
            in_specs.append(page_spec(width, p))
            operands.append(cache)
    vd = dv_w // n_kv
    kern = functools.partial(_paged_kernel, n_pages=npg, steps=steps, mla_scale=mla_scale,
                             lat_w=lat_w, rope_w=rope_w)
    grid_spec = pltpu.PrefetchScalarGridSpec(
        num_scalar_prefetch=1,
        grid=(nb, n_pages_total // npg),
        in_specs=in_specs,
        out_specs=[pl.BlockSpec((1, rows_m, lat_w), per_b3),
                   pl.BlockSpec((1, n_kv, rows_d, vd), per_b4)],
        scratch_shapes=[pltpu.VMEM((rows_m, 1), F32), pltpu.VMEM((rows_m, 1), F32),
                        pltpu.VMEM((rows_m, lat_w), F32),
                        pltpu.VMEM((n_kv, rows_d, 1), F32), pltpu.VMEM((n_kv, rows_d, 1), F32),
                        pltpu.VMEM((n_kv, rows_d, vd), F32)],
    )
    return pl.pallas_call(
        kern,
        grid_spec=grid_spec,
        out_shape=[jax.ShapeDtypeStruct((nb, rows_m, lat_w), F32),
                   jax.ShapeDtypeStruct((nb, n_kv, rows_d, vd), F32)],
        compiler_params=_cparams("parallel", "arbitrary"),
        name="paged",
    )(page_table, *operands)


def _rot_cols(w, seg):
    k, n = w.shape[-2:]
    w4 = w.reshape(w.shape[:-1] + (n // seg, 2, seg // 2))
    return jnp.stack([-w4[..., 1, :], w4[..., 0, :]], axis=-2).reshape(w.shape)


def _rope_tables(pos, d):
    half = d // 2
    inv = ROPE_THETA ** (-jnp.arange(half, dtype=F32) * 2.0 / d)
    ang = pos.astype(F32)[:, None] * inv[None, :]
    cos = jnp.cos(ang)
    sin = jnp.sin(ang)
    return jnp.concatenate([cos, cos], axis=-1), jnp.concatenate([sin, sin], axis=-1)


def _apply_rope(x, x_rot, cos, sin, seg):
    t, n = x.shape
    x3 = x.reshape(t, n // seg, seg)
    r3 = x_rot.reshape(t, n // seg, seg)
    return (x3 * cos[:, None, :] + r3 * sin[:, None, :]).reshape(t, n)


def _block_diag(blocks):
    g, r, c = blocks.shape
    eye = jnp.eye(g, dtype=blocks.dtype)
    return jnp.einsum('grc,gh->grhc', blocks, eye).reshape(g * r, g * c)


def _cplx_pow2(ar, ai, n_sq):
    for _ in range(n_sq):
        ar, ai = ar * ar - ai * ai, 2.0 * ar * ai
    return ar, ai


def kernel(x_prompt, x_sample, cache_mla_latent, cache_mla_krope, cache_diff_k, cache_diff_v, state_ssm_re, state_ssm_im, page_table, g_mix, w_in, b_gate, ssm_lambda_re, ssm_lambda_im, ssm_log_dt, ssm_b_re, ssm_b_im, ssm_c_re, ssm_c_im, ssm_d, ssm_w_glu, ssm_b_glu, mla_g_q, mla_w_uq, mla_g_kv, mla_w_uk, mla_w_uv, diff_lambda_q1, diff_lambda_k1, diff_lambda_q2, diff_lambda_k2, diff_g_sub, w_br_ssm, w_br_mla, w_br_diff, w_out, g_mlp, w_up, w_down, g_final):
    nbp, seq, dm = x_prompt.shape
    nbs, steps, _ = x_sample.shape
    depth = w_in.shape[0]
    n_pool, page = cache_mla_latent.shape[1:3]
    past_len = page_table.shape[1] * page
    ssm_w = ssm_w_glu.shape[1]
    n_grp = ssm_w // SSM_GROUP
    ns = n_grp * SSM_STATE
    q_lora = mla_g_q.shape[1]
    kv_lora = mla_g_kv.shape[1]
    mla_qd = MLA_NOPE + MLA_ROPE
    mla_scale = 1.0 / math.sqrt(mla_qd)
    diff_scale = 1.0 / math.sqrt(DIFF_HD)
    dq_w = DIFF_HEADS * 2 * DIFF_HD
    dk_w = DIFF_KV_HEADS * 2 * DIFF_HD
    dv_w = DIFF_KV_HEADS * DIFF_VD
    n_br = b_gate.shape[1]
    tp = nbp * seq
    ts = nbs * steps
    tt = tp + ts

    tm = 768 if tt % 768 == 0 else 256
    tile = min(256, seq)
    tl = min(512, seq)
    chunk_len = tl // SUBLANE

    sizes = (ssm_w, q_lora, kv_lora, MLA_ROPE, dq_w, dk_w, dv_w, n_br * dm)
    offs = [0]
    for s in sizes:
        offs.append(offs[-1] + s)
    col = lambda i: w_in[:, :, offs[i]:offs[i + 1]]
    w_kr = col(3)
    kr_pack = jnp.concatenate([w_kr, _rot_cols(w_kr, MLA_ROPE),
                               jnp.zeros((depth, dm, LANE - 2 * MLA_ROPE), F32)], axis=-1)
    w_ext = jnp.concatenate([col(0), col(1), col(2), kr_pack, col(4), _rot_cols(col(4), DIFF_HD),
                             col(5), _rot_cols(col(5), DIFF_HD), col(6), col(7)], axis=-1).astype(BF16)
    e_sizes = (ssm_w, q_lora, kv_lora, LANE, dq_w, dq_w, dk_w, dk_w, dv_w, n_br * dm)
    e_offs = [0]
    for s in e_sizes:
        e_offs.append(e_offs[-1] + s)

    step = jnp.exp(ssm_log_dt)[..., None]
    mag = jnp.exp(ssm_lambda_re * step)
    a_re = mag * jnp.cos(ssm_lambda_im * step)
    a_im = mag * jnp.sin(ssm_lambda_im * step)
    den = ssm_lambda_re * ssm_lambda_re + ssm_lambda_im * ssm_lambda_im
    z_re = ((a_re - 1.0) * ssm_lambda_re + a_im * ssm_lambda_im) / den
    z_im = (a_im * ssm_lambda_re - (a_re - 1.0) * ssm_lambda_im) / den
    bb_re = z_re[..., None] * ssm_b_re - z_im[..., None] * ssm_b_im
    bb_im = z_re[..., None] * ssm_b_im + z_im[..., None] * ssm_b_re
    bd = jax.vmap(_block_diag)
    wb = jnp.concatenate([bd(jnp.swapaxes(bb_re, -1, -2)), bd(jnp.swapaxes(bb_im, -1, -2))],
                         axis=-1).astype(BF16)
    wc = jnp.concatenate([bd(jnp.swapaxes(ssm_c_re, -1, -2)), -bd(jnp.swapaxes(ssm_c_im, -1, -2))],
                         axis=-2).astype(BF16)
    a_vec = jnp.stack([a_re.reshape(depth, ns), a_im.reshape(depth, ns)], axis=1).reshape(depth, 1, 2 * ns)
    p_re, p_im = _cplx_pow2(a_re, a_im, int(math.log2(chunk_len)))
    a_pow = jnp.stack([p_re.reshape(depth, ns), p_im.reshape(depth, ns)], axis=1).reshape(depth, 1, 2 * ns)
    d_vec = ssm_d.reshape(depth, 1, ssm_w)
    w_glu = ssm_w_glu.astype(BF16)

    w_uq_nope = mla_w_uq.reshape(depth, q_lora, MLA_HEADS, mla_qd)[..., :MLA_NOPE].reshape(depth, q_lora, -1)
    w_uq_rope = mla_w_uq.reshape(depth, q_lora, MLA_HEADS, mla_qd)[..., MLA_NOPE:].reshape(depth, q_lora, -1)
    w_uq_ext = jnp.concatenate([w_uq_nope, w_uq_rope, _rot_cols(w_uq_rope, MLA_ROPE)], axis=-1).astype(BF16)
    w_uk_bd = bd(jnp.transpose(mla_w_uk, (0, 2, 3, 1))).astype(BF16)
    w_uv_bd = bd(jnp.transpose(mla_w_uv, (0, 2, 1, 3))).astype(BF16)

    lam = (jnp.exp(jnp.sum(diff_lambda_q1 * diff_lambda_k1, axis=-1))
           - jnp.exp(jnp.sum(diff_lambda_q2 * diff_lambda_k2, axis=-1)))
    w_bs, w_bm, w_bd_, w_o = (w.astype(BF16) for w in (w_br_ssm, w_br_mla, w_br_diff, w_out))
    w_u, w_dn = w_up.astype(BF16), w_down.astype(BF16)
    bg = b_gate.reshape(depth, 1, n_br * dm)

    pos = jnp.concatenate([jnp.tile(jnp.arange(seq), nbp), jnp.tile(past_len + jnp.arange(steps), nbs)])
    cos32, sin32 = _rope_tables(pos, MLA_ROPE)
    cos64, sin64 = _rope_tables(pos, DIFF_HD)

    cache_dk2 = cache_diff_k.reshape(depth, n_pool, page, dk_w)
    cache_dv2 = cache_diff_v.reshape(depth, n_pool, page, dv_w)
    s0_grp = jnp.concatenate([state_ssm_re.reshape(depth, nbs, ns), state_ssm_im.reshape(depth, nbs, ns)],
                             axis=-1).reshape(depth, nbs // SUBLANE, SUBLANE, 2 * ns)

    x = jnp.concatenate([x_prompt.reshape(tp, dm), x_sample.reshape(ts, dm)], axis=0)
    rows = [[] for _ in range(12)]
    for l in range(depth):
        z = _norm_mm(x, g_mix[l][None], w_ext[l], tm, 2)
        zc = lambda i: z[:, e_offs[i]:e_offs[i + 1]]
        u, cq, ckv, krp, dq, dq_r, dk, dk_r, dv, gz = (zc(i) for i in range(10))

        u_p = u[:tp].reshape(nbp, seq // tl, SUBLANE, chunk_len, ssm_w)
        u_p = jnp.swapaxes(u_p, 2, 3).reshape(nbp, seq, ssm_w)
        y_p, fin_p = _ssm_prompt(u_p, wb[l], a_vec[l], a_pow[l], wc[l], d_vec[l], tl)
        y_p = jnp.swapaxes(y_p.reshape(nbp, seq // tl, chunk_len, SUBLANE, ssm_w), 2, 3).reshape(tp, ssm_w)
        u_s = u[tp:].reshape(nbs // SUBLANE, SUBLANE, steps, ssm_w)
        u_s = jnp.swapaxes(u_s, 1, 2).reshape(nbs // SUBLANE, steps * SUBLANE, ssm_w)
        y_s, fin_s = _ssm_sample(u_s, s0_grp[l], wb[l], a_vec[l], wc[l], d_vec[l], steps)
        y_s = jnp.swapaxes(y_s.reshape(nbs // SUBLANE, steps, SUBLANE, ssm_w), 1, 2).reshape(ts, ssm_w)
        y = jax.nn.gelu(jnp.concatenate([y_p, y_s], axis=0))
        y_ssm = y * jax.nn.sigmoid(_mm(y, w_glu[l], tm) + ssm_b_glu[l])
        fin_p = fin_p.reshape(nbp, 2, n_grp, SSM_STATE)
        fin_s = fin_s.reshape(nbs, 2, n_grp, SSM_STATE)

        q = _mm(_rms(cq, mla_g_q[l]), w_uq_ext[l], tm)
        n_nope = MLA_HEADS * MLA_NOPE
        n_rope = MLA_HEADS * MLA_ROPE
        q_lat = _mm(q[:, :n_nope], w_uk_bd[l], tm)
        q_rope = _apply_rope(q[:, n_nope:n_nope + n_rope], q[:, n_nope + n_rope:], cos32, sin32, MLA_ROPE)
        q_pad = jnp.concatenate([q_lat.reshape(tt, MLA_HEADS, kv_lora), q_rope.reshape(tt, MLA_HEADS, MLA_ROPE),
                                 jnp.zeros((tt, MLA_HEADS, LANE - MLA_ROPE), F32)], axis=-1).astype(BF16)
        ckv_n = _rms(ckv, mla_g_kv[l])
        kr = krp[:, :MLA_ROPE] * cos32 + krp[:, MLA_ROPE:2 * MLA_ROPE] * sin32
        k_pad = jnp.concatenate([ckv_n, kr, jnp.zeros((tt, LANE - MLA_ROPE), F32)], axis=-1).astype(BF16)
        dq_pad_w = kv_lora + LANE

        qp = jnp.transpose(q_pad[:tp].reshape(nbp, seq, MLA_HEADS, dq_pad_w), (0, 2, 1, 3))
        ktp = jnp.swapaxes(k_pad[:tp].reshape(nbp, seq // tile, tile, dq_pad_w), 2, 3)
        vp = ckv_n[:tp].astype(BF16).reshape(nbp, seq // tile, tile, kv_lora)
        o_p = _flash(qp, ktp, vp, tile, mla_scale)
        o_p = jnp.transpose(o_p, (0, 2, 1, 3)).reshape(tp, MLA_HEADS * kv_lora)

        qd = _apply_rope(dq, dq_r, cos64, sin64, DIFF_HD) * diff_scale
        kd = _apply_rope(dk, dk_r, cos64, sin64, DIFF_HD)
        qd6 = qd.reshape(tt, DIFF_KV_HEADS, DIFF_GROUP, 2, DIFF_HD)
        zeros = jnp.zeros_like(qd6[..., 0, :])
        qd_blk = jnp.stack([jnp.concatenate([qd6[..., 0, :], zeros], axis=-1),
                            jnp.concatenate([zeros, qd6[..., 1, :]], axis=-1)], axis=-2).astype(BF16)
        qdp = jnp.transpose(qd_blk[:tp].reshape(nbp, seq, DIFF_KV_HEADS, DIFF_GROUP * 2, 2 * DIFF_HD),
                            (0, 2, 3, 1, 4)).reshape(nbp * DIFF_KV_HEADS, DIFF_GROUP * 2, seq, 2 * DIFF_HD)
        kdp = jnp.transpose(kd[:tp].astype(BF16).reshape(nbp, seq // tile, tile, DIFF_KV_HEADS, 2 * DIFF_HD),
                            (0, 3, 1, 4, 2)).reshape(nbp * DIFF_KV_HEADS, seq // tile, 2 * DIFF_HD, tile)
        vdp = jnp.transpose(dv[:tp].astype(BF16).reshape(nbp, seq // tile, tile, DIFF_KV_HEADS, DIFF_VD),
                            (0, 3, 1, 2, 4)).reshape(nbp * DIFF_KV_HEADS, seq // tile, tile, DIFF_VD)
        od_p = _flash(qdp, kdp, vdp, tile, 1.0)
        od_p = jnp.transpose(od_p.reshape(nbp, DIFF_KV_HEADS, DIFF_GROUP, 2, seq, DIFF_VD), (0, 4, 1, 2, 3, 5))
        od_p = od_p.reshape(tp, DIFF_HEADS, 2, DIFF_VD)

        pad8 = lambda a: jnp.pad(a.reshape(nbs, steps, -1), ((0, 0), (0, page - steps), (0, 0)))
        qm_s = jnp.transpose(q_pad[tp:].reshape(nbs, steps, MLA_HEADS, dq_pad_w), (0, 2, 1, 3))
        qm_s = qm_s.reshape(nbs, MLA_HEADS * steps, dq_pad_w)
        qd_s = jnp.transpose(qd_blk[tp:].reshape(nbs, steps, DIFF_KV_HEADS, DIFF_GROUP * 2, 2 * DIFF_HD),
                             (0, 2, 3, 1, 4)).reshape(nbs, DIFF_KV_HEADS, DIFF_GROUP * 2 * steps, 2 * DIFF_HD)
        o_s, od_s = _paged(page_table, l, qm_s, qd_s, pad8(ckv_n[tp:]), pad8(kr[tp:]), pad8(kd[tp:]), pad8(dv[tp:]),
                           cache_mla_latent, cache_mla_krope, cache_dk2, cache_dv2, steps, mla_scale)
        o_s = jnp.transpose(o_s.reshape(nbs, MLA_HEADS, steps, kv_lora), (0, 2, 1, 3)).reshape(ts, MLA_HEADS * kv_lora)
        od_s = jnp.transpose(od_s.reshape(nbs, DIFF_KV_HEADS, DIFF_GROUP, 2, steps, DIFF_VD), (0, 4, 1, 2, 3, 5))
        od_s = od_s.reshape(ts, DIFF_HEADS, 2, DIFF_VD)

        y_mla = _mm(jnp.concatenate([o_p, o_s], axis=0), w_uv_bd[l], tm)
        lam_init = 0.8 - 0.6 * math.exp(-0.3 * l)
        lam_full = lam[l] + lam_init
        od = jnp.concatenate([od_p, od_s], axis=0)
        od = od[:, :, 0, :] - lam_full * od[:, :, 1, :]
        y_diff = (_rms(od, diff_g_sub[l]) * (1.0 - lam_init)).reshape(tt, DIFF_HEADS * DIFF_VD)

        x = _merge(x, y_ssm, y_mla, y_diff, gz, bg[l], w_bs[l], w_bm[l], w_bd_[l], w_o[l], tm)
        x = _mlp(x, g_mlp[l][None], w_u[l], w_dn[l], tm, 1024)

        new = (ckv_n, kr, kd, dv)
        shapes = ((kv_lora,), (MLA_ROPE,), (DIFF_KV_HEADS, 2, DIFF_HD), (DIFF_KV_HEADS, DIFF_VD))
        for i, (a, shp) in enumerate(zip(new, shapes)):
            rows[i].append(a[:tp].reshape((nbp, seq) + shp))
            rows[6 + i].append(a[tp:].reshape((nbs, steps) + shp))
        rows[4].append(fin_p[:, 0])
        rows[5].append(fin_p[:, 1])
        rows[10].append(fin_s[:, 0])
        rows[11].append(fin_s[:, 1])

    y = _norm(x, g_final[None], tm)
    outs = [jnp.stack(r) for r in rows]
    return (y[:tp].reshape(nbp, seq, dm), y[tp:].reshape(nbs, steps, dm), *outs)
```

```python
import functools
import math

import jax
import jax.numpy as jnp
from jax import lax
from jax.experimental import pallas as pl
from jax.experimental.pallas import tpu as pltpu

F32 = jnp.float32
BF16 = jnp.bfloat16

SSM_GROUP = 16
SSM_STATE = 64
MLA_HEADS = 8
MLA_NOPE = 64
MLA_ROPE = 32
DIFF_HEADS = 4
DIFF_KV_HEADS = 2
DIFF_GROUP = DIFF_HEADS // DIFF_KV_HEADS
DIFF_HD = 64
DIFF_VD = 2 * DIFF_HD
ROPE_THETA = 10000.0
EPS = 1e-6

LANE = 128
SUBLANE = 8
VMEM_LIMIT = 56 * 1024 * 1024


def _cparams(*sem):
    return pltpu.CompilerParams(dimension_semantics=sem, vmem_limit_bytes=VMEM_LIMIT)


def _rms(x, g):
    ms = jnp.mean(x * x, axis=-1, keepdims=True)
    return x * lax.rsqrt(ms + EPS) * g


def _mm_kernel(x_ref, w_ref, o_ref):
    o_ref[...] = jnp.dot(x_ref[...].astype(BF16), w_ref[...],
                         preferred_element_type=F32).astype(o_ref.dtype)


def _mm(x, w, tm, out_dtype=F32):
    t, k = x.shape
    n = w.shape[1]
    return pl.pallas_call(
        _mm_kernel,
        grid=(t // tm,),
        in_specs=[pl.BlockSpec((tm, k), lambda i: (i, 0)),
                  pl.BlockSpec((k, n), lambda i: (0, 0))],
        out_specs=pl.BlockSpec((tm, n), lambda i: (i, 0)),
        out_shape=jax.ShapeDtypeStruct((t, n), out_dtype),
        compiler_params=_cparams("parallel"),
        name="mm",
    )(x, w)


def _norm_mm_kernel(x_ref, g_ref, w_ref, o_ref):
    h = _rms(x_ref[...], g_ref[...]).astype(BF16)
    o_ref[...] = jnp.dot(h, w_ref[...], preferred_element_type=F32)


def _norm_mm(x, g, w, tm, n_col_blocks):
    t, k = x.shape
    n = w.shape[1]
    tn = n // n_col_blocks
    return pl.pallas_call(
        _norm_mm_kernel,
        grid=(n_col_blocks, t // tm),
        in_specs=[pl.BlockSpec((tm, k), lambda j, i: (i, 0)),
                  pl.BlockSpec((1, k), lambda j, i: (0, 0)),
                  pl.BlockSpec((k, tn), lambda j, i: (0, j))],
        out_specs=pl.BlockSpec((tm, tn), lambda j, i: (i, j)),
        out_shape=jax.ShapeDtypeStruct((t, n), F32),
        compiler_params=_cparams("parallel", "parallel"),
        name="norm_mm",
    )(x, g, w)


def _merge_kernel(x_ref, ys_ref, ym_ref, yd_ref, gz_ref, bg_ref, ws_ref, wm_ref, wd_ref, wo_ref, o_ref):
    d = x_ref.shape[1]
    merged = None
    for b, (y_ref, w_ref) in enumerate(((ys_ref, ws_ref), (ym_ref, wm_ref), (yd_ref, wd_ref))):
        gate = jax.nn.sigmoid(gz_ref[:, b * d:(b + 1) * d] + bg_ref[:, b * d:(b + 1) * d])
        term = gate * jnp.dot(y_ref[...].astype(BF16), w_ref[...], preferred_element_type=F32)
        merged = term if merged is None else merged + term
    o_ref[...] = x_ref[...] + jnp.dot(merged.astype(BF16), wo_ref[...], preferred_element_type=F32)


def _merge(x, y_ssm, y_mla, y_diff, gz, b_gate, w_s, w_m, w_d, w_o, tm):
    t, d = x.shape
    row = lambda i: (i, 0)
    const = lambda i: (0, 0)
    return pl.pallas_call(
        _merge_kernel,
        grid=(t // tm,),
        in_specs=[pl.BlockSpec((tm, d), row),
                  pl.BlockSpec((tm, y_ssm.shape[1]), row),
                  pl.BlockSpec((tm, y_mla.shape[1]), row),
                  pl.BlockSpec((tm, y_diff.shape[1]), row),
                  pl.BlockSpec((tm, gz.shape[1]), row),
                  pl.BlockSpec(b_gate.shape, const),
                  pl.BlockSpec(w_s.shape, const),
                  pl.BlockSpec(w_m.shape, const),
                  pl.BlockSpec(w_d.shape, const),
                  pl.BlockSpec(w_o.shape, const)],
        out_specs=pl.BlockSpec((tm, d), row),
        out_shape=jax.ShapeDtypeStruct((t, d), F32),
        compiler_params=_cparams("parallel"),
        name="merge",
    )(x, y_ssm, y_mla, y_diff, gz, b_gate, w_s, w_m, w_d, w_o)


def _mlp_kernel(x_ref, g_ref, wu_ref, wd_ref, o_ref, h_sc):
    f = pl.program_id(1)

    @pl.when(f == 0)
    def _():
        x = x_ref[...]
        h_sc[...] = _rms(x, g_ref[...]).astype(BF16)
        o_ref[...] = x

    a = jnp.dot(h_sc[...], wu_ref[...], preferred_element_type=F32)
    a = jnp.square(jnp.maximum(a, 0.0)).astype(BF16)
    o_ref[...] += jnp.dot(a, wd_ref[...], preferred_element_type=F32)


def _mlp(x, g, w_up, w_down, tm, tf):
    t, d = x.shape
    ff = w_up.shape[1]
    return pl.pallas_call(
        _mlp_kernel,
        grid=(t // tm, ff // tf),
        in_specs=[pl.BlockSpec((tm, d), lambda i, f: (i, 0)),
                  pl.BlockSpec((1, d), lambda i, f: (0, 0)),
                  pl.BlockSpec((d, tf), lambda i, f: (0, f)),
                  pl.BlockSpec((tf, d), lambda i, f: (f, 0))],
        out_specs=pl.BlockSpec((tm, d), lambda i, f: (i, 0)),
        out_shape=jax.ShapeDtypeStruct((t, d), F32),
        scratch_shapes=[pltpu.VMEM((tm, d), BF16)],
        compiler_params=_cparams("parallel", "arbitrary"),
        name="mlp",
    )(x, g, w_up, w_down)


def _norm_kernel(x_ref, g_ref, o_ref):
    o_ref[...] = _rms(x_ref[...], g_ref[...])


def _norm(x, g, tm):
    t, d = x.shape
    return pl.pallas_call(
        _norm_kernel,
        grid=(t // tm,),
        in_specs=[pl.BlockSpec((tm, d), lambda i: (i, 0)), pl.BlockSpec((1, d), lambda i: (0, 0))],
        out_specs=pl.BlockSpec((tm, d), lambda i: (i, 0)),
        out_shape=jax.ShapeDtypeStruct((t, d), F32),
        compiler_params=_cparams("parallel"),
        name="final_norm",
    )(x, g)


SCAN_COLS = 4 * LANE


def _cplx_step(ar, ai, sr, si, xr, xi):
    return ar * sr - ai * si + xr, ar * si + ai * sr + xi


def _ssm_prompt_kernel(u_ref, wb_ref, a_ref, ap_ref, wc_ref, d_ref, y_ref, fin_ref,
                       xs, state_sc, carry_sc, fin_sc, *, chunk_len, ns):
    li = pl.program_id(1)

    @pl.when(li == 0)
    def _():
        state_sc[...] = jnp.zeros_like(state_sc)

    u = u_ref[0]
    xs[...] = jnp.dot(u.astype(BF16), wb_ref[...], preferred_element_type=F32)

    n_cb = ns // SCAN_COLS
    for cb in range(n_cb):
        re = pl.ds(cb * SCAN_COLS, SCAN_COLS)
        im = pl.ds(ns + cb * SCAN_COLS, SCAN_COLS)
        ar = jnp.broadcast_to(a_ref[0:1, re], (SUBLANE, SCAN_COLS))
        ai = jnp.broadcast_to(a_ref[0:1, im], (SUBLANE, SCAN_COLS))

        def sweep1(t, s):
            rows = pl.ds(pl.multiple_of(t * SUBLANE, SUBLANE), SUBLANE)
            return _cplx_step(ar, ai, s[0], s[1], xs[rows, re], xs[rows, im])

        zero = jnp.zeros((SUBLANE, SCAN_COLS), F32)
        fr, fi = lax.fori_loop(0, chunk_len, sweep1, (zero, zero))
        fin_sc[:, re] = fr
        fin_sc[:, im] = fi

    carry_sc[0:1, :] = state_sc[0:1, :]
    pr = ap_ref[0:1, 0:ns]
    pi = ap_ref[0:1, ns:2 * ns]
    for c in range(1, SUBLANE):
        cr = carry_sc[c - 1:c, 0:ns]
        ci = carry_sc[c - 1:c, ns:2 * ns]
        nr, ni = _cplx_step(pr, pi, cr, ci, fin_sc[c - 1:c, 0:ns], fin_sc[c - 1:c, ns:2 * ns])
        carry_sc[c:c + 1, 0:ns] = nr
        carry_sc[c:c + 1, ns:2 * ns] = ni

    for cb in range(n_cb):
        re = pl.ds(cb * SCAN_COLS, SCAN_COLS)
        im = pl.ds(ns + cb * SCAN_COLS, SCAN_COLS)
        ar = jnp.broadcast_to(a_ref[0:1, re], (SUBLANE, SCAN_COLS))
        ai = jnp.broadcast_to(a_ref[0:1, im], (SUBLANE, SCAN_COLS))

        def sweep2(t, s):
            rows = pl.ds(pl.multiple_of(t * SUBLANE, SUBLANE), SUBLANE)
            nr, ni = _cplx_step(ar, ai, s[0], s[1], xs[rows, re], xs[rows, im])
            xs[rows, re] = nr
            xs[rows, im] = ni
            return nr, ni

        fr, fi = lax.fori_loop(0, chunk_len, sweep2, (carry_sc[:, re], carry_sc[:, im]))
        state_sc[0:1, re] = fr[SUBLANE - 1:SUBLANE]
        state_sc[0:1, im] = fi[SUBLANE - 1:SUBLANE]

    y_ref[0] = jnp.dot(xs[...].astype(BF16), wc_ref[...], preferred_element_type=F32) + d_ref[...] * u
    fin_ref[0] = state_sc[0:1, :]


def _ssm_prompt(u_perm, wb, a_vec, a_pow, wc, d_vec, tl):
    nb, seq, w = u_perm.shape
    ns2 = wb.shape[1]
    const = lambda b, i: (0, 0)
    kern = functools.partial(_ssm_prompt_kernel, chunk_len=tl // SUBLANE, ns=ns2 // 2)
    return pl.pallas_call(
        kern,
        grid=(nb, seq // tl),
        in_specs=[pl.BlockSpec((1, tl, w), lambda b, i: (b, i, 0)),
                  pl.BlockSpec(wb.shape, const),
                  pl.BlockSpec(a_vec.shape, const),
                  pl.BlockSpec(a_pow.shape, const),
                  pl.BlockSpec(wc.shape, const),
                  pl.BlockSpec(d_vec.shape, const)],
        out_specs=[pl.BlockSpec((1, tl, w), lambda b, i: (b, i, 0)),
                   pl.BlockSpec((1, 1, ns2), lambda b, i: (b, 0, 0))],
        out_shape=[jax.ShapeDtypeStruct((nb, seq, w), F32),
                   jax.ShapeDtypeStruct((nb, 1, ns2), F32)],
        scratch_shapes=[pltpu.VMEM((tl, ns2), F32),
                        pltpu.VMEM((SUBLANE, ns2), F32),
                        pltpu.VMEM((SUBLANE, ns2), F32),
                        pltpu.VMEM((SUBLANE, ns2), F32)],
        compiler_params=_cparams("parallel", "arbitrary"),
        name="ssm_prompt",
    )(u_perm, wb, a_vec, a_pow, wc, d_vec)


def _ssm_sample_kernel(u_ref, s0_ref, wb_ref, a_ref, wc_ref, d_ref, y_ref, fin_ref, xs, *, steps, ns):
    u = u_ref[0]
    xs[...] = jnp.dot(u.astype(BF16), wb_ref[...], preferred_element_type=F32)
    for cb in range(ns // SCAN_COLS):
        re = pl.ds(cb * SCAN_COLS, SCAN_COLS)
        im = pl.ds(ns + cb * SCAN_COLS, SCAN_COLS)
        ar = jnp.broadcast_to(a_ref[0:1, re], (SUBLANE, SCAN_COLS))
        ai = jnp.broadcast_to(a_ref[0:1, im], (SUBLANE, SCAN_COLS))
        sr = s0_ref[0, :, re]
        si = s0_ref[0, :, im]
        for t in range(steps):
            rows = pl.ds(t * SUBLANE, SUBLANE)
            sr, si = _cplx_step(ar, ai, sr, si, xs[rows, re], xs[rows, im])
            xs[rows, re] = sr
            xs[rows, im] = si
        fin_ref[0, :, re] = sr
        fin_ref[0, :, im] = si
    y_ref[0] = jnp.dot(xs[...].astype(BF16), wc_ref[...], preferred_element_type=F32) + d_ref[...] * u


def _ssm_sample(u_grp, s0_grp, wb, a_vec, wc, d_vec, steps):
    ng, rows, w = u_grp.shape
    ns2 = wb.shape[1]
    const = lambda g: (0, 0)
    kern = functools.partial(_ssm_sample_kernel, steps=steps, ns=ns2 // 2)
    return pl.pallas_call(
        kern,
        grid=(ng,),
        in_specs=[pl.BlockSpec((1, rows, w), lambda g: (g, 0, 0)),
                  pl.BlockSpec((1, SUBLANE, ns2), lambda g: (g, 0, 0)),
                  pl.BlockSpec(wb.shape, const),
                  pl.BlockSpec(a_vec.shape, const),
                  pl.BlockSpec(wc.shape, const),
                  pl.BlockSpec(d_vec.shape, const)],
        out_specs=[pl.BlockSpec((1, rows, w), lambda g: (g, 0, 0)),
                   pl.BlockSpec((1, SUBLANE, ns2), lambda g: (g, 0, 0))],
        out_shape=[jax.ShapeDtypeStruct((ng, rows, w), F32),
                   jax.ShapeDtypeStruct((ng, SUBLANE, ns2), F32)],
        scratch_shapes=[pltpu.VMEM((rows, ns2), F32)],
        compiler_params=_cparams("parallel"),
        name="ssm_sample",
    )(u_grp, s0_grp, wb, a_vec, wc, d_vec)


def _softmax_step(s, v, m_sc, l_sc, acc_sc):
    reps = s.shape[1] // LANE
    m_prev = m_sc[...]
    m_next = jnp.maximum(m_prev, jnp.max(s, axis=1, keepdims=True))
    p = jnp.exp2(s - pltpu.repeat(m_next, reps, axis=1))
    alpha = jnp.exp2(m_prev - m_next)
    l_sc[...] = alpha * l_sc[...] + jnp.sum(p, axis=1, keepdims=True)
    acc_sc[...] = alpha * acc_sc[...] + jnp.dot(p.astype(BF16), v, preferred_element_type=F32)
    m_sc[...] = m_next


def _flash_kernel(q_ref, kt_ref, v_ref, o_ref, m_sc, l_sc, acc_sc):
    _, h, tq, d = q_ref.shape
    tk = kt_ref.shape[3]
    i = pl.program_id(1)
    q = q_ref[0].reshape(h * tq, d)
    m_sc[...] = jnp.full_like(m_sc, -jnp.inf)
    l_sc[...] = jnp.zeros_like(l_sc)
    acc_sc[...] = jnp.zeros_like(acc_sc)
    n_full = (i * tq) // tk
    first_key = n_full * tk

    def body(j, carry):
        s = jnp.dot(q, kt_ref[0, j], preferred_element_type=F32)
        _softmax_step(s, v_ref[0, j], m_sc, l_sc, acc_sc)
        return carry

    lax.fori_loop(0, n_full, body, 0)

    s = jnp.dot(q, kt_ref[0, n_full], preferred_element_type=F32).reshape(h, tq, tk)
    row = lax.broadcasted_iota(jnp.int32, (tq, tk), 0) + (i * tq - first_key)
    col = lax.broadcasted_iota(jnp.int32, (tq, tk), 1)
    s = jnp.where((col <= row)[None], s, -jnp.inf).reshape(h * tq, tk)
    _softmax_step(s, v_ref[0, n_full], m_sc, l_sc, acc_sc)
    o_ref[0] = (acc_sc[...] / l_sc[...]).reshape(h, tq, LANE)


def _flash(q, kt, v, tq):
    nb, h, seq, d = q.shape
    return pl.pallas_call(
        _flash_kernel,
        grid=(nb, seq // tq),
        in_specs=[pl.BlockSpec((1, h, tq, d), lambda b, i: (b, 0, i, 0)),
                  pl.BlockSpec((1,) + kt.shape[1:], lambda b, i: (b, 0, 0, 0)),
                  pl.BlockSpec((1,) + v.shape[1:], lambda b, i: (b, 0, 0, 0))],
        out_specs=pl.BlockSpec((1, h, tq, LANE), lambda b, i: (b, 0, i, 0)),
        out_shape=jax.ShapeDtypeStruct((nb, h, seq, LANE), F32),
        scratch_shapes=[pltpu.VMEM((h * tq, LANE), F32),
                        pltpu.VMEM((h * tq, LANE), F32),
                        pltpu.VMEM((h * tq, LANE), F32)],
        compiler_params=_cparams("parallel", "parallel"),
        name="flash",
    )(q, kt, v)


PAGES_PER_STEP = 16
LOG2E = math.log2(math.e)


def _nt_dot(a, b):
    return lax.dot_general(a, b, (((1,), (1,)), ((), ())), preferred_element_type=F32)


def _paged_kernel(pt_ref, ql_ref, qr_ref, qd_ref, nlat_ref, nkr_ref, ndk_ref, ndv_ref, *rest,
                  n_pages, steps):
    pages = rest[:4 * n_pages]
    o_ref, m_sc, l_sc, acc_sc = rest[4 * n_pages:]
    g = pl.program_id(1)
    n_kv = qd_ref.shape[1]
    rows_m = ql_ref.shape[1]
    rows_d = qd_ref.shape[2]
    page = nlat_ref.shape[1]
    hd2 = qd_ref.shape[3]

    @pl.when(g == 0)
    def _():
        m_sc[...] = jnp.full_like(m_sc, -jnp.inf)
        l_sc[...] = jnp.zeros_like(l_sc)
        acc_sc[...] = jnp.zeros_like(acc_sc)

    def attend(lat_refs, kr_refs, dk_refs, dv_refs, mask):
        cat = lambda xs, axis: xs[0] if len(xs) == 1 else jnp.concatenate(xs, axis=axis)
        lat = cat([r[...].astype(BF16) for r in lat_refs], 0)
        krt = cat([r[...].astype(BF16) for r in kr_refs], 1)
        scores = [_nt_dot(ql_ref[0], lat) + jnp.dot(qr_ref[0], krt, preferred_element_type=F32)]
        values = [lat]
        for kv in range(n_kv):
            dkt = cat([r[kv * hd2:(kv + 1) * hd2, :].astype(BF16) for r in dk_refs], 1)
            scores.append(jnp.dot(qd_ref[0, kv], dkt, preferred_element_type=F32))
            values.append(cat([r[pl.ds(kv, page, stride=n_kv), :].astype(BF16) for r in dv_refs], 0))
        s = jnp.concatenate(scores, axis=0)
        if mask:
            row = lax.broadcasted_iota(jnp.int32, s.shape, 0) % steps
            col = lax.broadcasted_iota(jnp.int32, s.shape, 1)
            s = jnp.where(col <= row, s, -jnp.inf)
        reps = s.shape[1] // LANE
        m_prev = m_sc[...]
        m_next = jnp.maximum(m_prev, jnp.max(s, axis=1, keepdims=True))
        p = jnp.exp2(s - pltpu.repeat(m_next, reps, axis=1))
        alpha = jnp.exp2(m_prev - m_next)
        l_sc[...] = alpha * l_sc[...] + jnp.sum(p, axis=1, keepdims=True)
        p = p.astype(BF16)
        bounds = [0, rows_m] + [rows_m + (kv + 1) * rows_d for kv in range(n_kv)]
        pv = [jnp.dot(p[lo:hi], v, preferred_element_type=F32)
              for lo, hi, v in zip(bounds[:-1], bounds[1:], values)]
        acc_sc[...] = alpha * acc_sc[...] + jnp.concatenate(pv, axis=0)
        m_sc[...] = m_next

    attend(pages[0:n_pages], pages[n_pages:2 * n_pages], pages[2 * n_pages:3 * n_pages],
           pages[3 * n_pages:4 * n_pages], False)

    @pl.when(g == pl.num_programs(1) - 1)
    def _():
        attend([nlat_ref.at[0]], [nkr_ref.at[0]], [ndk_ref.at[0]], [ndv_ref.at[0]], True)
        o_ref[0] = acc_sc[...] / l_sc[...]


def _paged(page_table, layer, q_lat, q_rope, q_diff, new_lat, new_krt, new_dkt, new_dv,
           cache_lat, cache_krt, cache_dkt, cache_dv, steps):
    nb, n_pages_total = page_table.shape
    npg = math.gcd(PAGES_PER_STEP, n_pages_total)
    n_kv = q_diff.shape[1]
    rows = q_lat.shape[1] + n_kv * q_diff.shape[2]

    def page_spec(cache, p):
        return pl.BlockSpec((None, None) + cache.shape[2:],
                            lambda b, g, pt: (layer, pt[b, g * npg + p], 0, 0))

    per_b3 = lambda b, g, pt: (b, 0, 0)
    per_b4 = lambda b, g, pt: (b, 0, 0, 0)
    operands = [q_lat, q_rope, q_diff, new_lat, new_krt, new_dkt, new_dv]
    in_specs = [pl.BlockSpec((1,) + a.shape[1:], per_b4 if a.ndim == 4 else per_b3) for a in operands]
    for cache in (cache_lat, cache_krt, cache_dkt, cache_dv):
        for p in range(npg):
            in_specs.append(page_spec(cache, p))
            operands.append(cache)
    kern = functools.partial(_paged_kernel, n_pages=npg, steps=steps)
    grid_spec = pltpu.PrefetchScalarGridSpec(
        num_scalar_prefetch=1,
        grid=(nb, n_pages_total // npg),
        in_specs=in_specs,
        out_specs=pl.BlockSpec((1, rows, LANE), per_b3),
        scratch_shapes=[pltpu.VMEM((rows, LANE), F32)] * 3,
    )
    return pl.pallas_call(
        kern,
        grid_spec=grid_spec,
        out_shape=jax.ShapeDtypeStruct((nb, rows, LANE), F32),
        compiler_params=_cparams("parallel", "arbitrary"),
        name="paged",
    )(page_table, *operands)


def _rot_cols(w, seg):
    k, n = w.shape[-2:]
    w4 = w.reshape(w.shape[:-1] + (n // seg, 2, seg // 2))
    return jnp.stack([-w4[..., 1, :], w4[..., 0, :]], axis=-2).reshape(w.shape)


def _rope_tables(pos, d):
    half = d // 2
    inv = ROPE_THETA ** (-jnp.arange(half, dtype=F32) * 2.0 / d)
    ang = pos.astype(F32)[:, None] * inv[None, :]
    cos = jnp.cos(ang)
    sin = jnp.sin(ang)
    return jnp.concatenate([cos, cos], axis=-1), jnp.concatenate([sin, sin], axis=-1)


def _apply_rope(x, x_rot, cos, sin, seg):
    t, n = x.shape
    x3 = x.reshape(t, n // seg, seg)
    r3 = x_rot.reshape(t, n // seg, seg)
    return (x3 * cos[:, None, :] + r3 * sin[:, None, :]).reshape(t, n)


def _block_diag(blocks):
    g, r, c = blocks.shape
    eye = jnp.eye(g, dtype=blocks.dtype)
    return jnp.einsum('grc,gh->grhc', blocks, eye).reshape(g * r, g * c)


def _cplx_pow2(ar, ai, n_sq):
    for _ in range(n_sq):
        ar, ai = ar * ar - ai * ai, 2.0 * ar * ai
    return ar, ai


def kernel(x_prompt, x_sample, cache_mla_latent, cache_mla_krope, cache_diff_k, cache_diff_v, state_ssm_re, state_ssm_im, page_table, g_mix, w_in, b_gate, ssm_lambda_re, ssm_lambda_im, ssm_log_dt, ssm_b_re, ssm_b_im, ssm_c_re, ssm_c_im, ssm_d, ssm_w_glu, ssm_b_glu, mla_g_q, mla_w_uq, mla_g_kv, mla_w_uk, mla_w_uv, diff_lambda_q1, diff_lambda_k1, diff_lambda_q2, diff_lambda_k2, diff_g_sub, w_br_ssm, w_br_mla, w_br_diff, w_out, g_mlp, w_up, w_down, g_final):
    nbp, seq, dm = x_prompt.shape
    nbs, steps, _ = x_sample.shape
    depth = w_in.shape[0]
    n_pool, page = cache_mla_latent.shape[1:3]
    past_len = page_table.shape[1] * page
    ssm_w = ssm_w_glu.shape[1]
    n_grp = ssm_w // SSM_GROUP
    ns = n_grp * SSM_STATE
    q_lora = mla_g_q.shape[1]
    kv_lora = mla_g_kv.shape[1]
    mla_qd = MLA_NOPE + MLA_ROPE
    mla_scale = 1.0 / math.sqrt(mla_qd)
    diff_scale = 1.0 / math.sqrt(DIFF_HD)
    dq_w = DIFF_HEADS * 2 * DIFF_HD
    dk_w = DIFF_KV_HEADS * 2 * DIFF_HD
    dv_w = DIFF_KV_HEADS * DIFF_VD
    n_br = b_gate.shape[1]
    tp = nbp * seq
    ts = nbs * steps
    tt = tp + ts

    tm = 768 if tt % 768 == 0 else 256
    tq = min(256, seq)
    tk = min(512, seq)
    tl = min(512, seq)
    chunk_len = tl // SUBLANE

    sizes = (ssm_w, q_lora, kv_lora, MLA_ROPE, dq_w, dk_w, dv_w, n_br * dm)
    offs = [0]
    for s in sizes:
        offs.append(offs[-1] + s)
    col = lambda i: w_in[:, :, offs[i]:offs[i + 1]]
    w_kr = col(3)
    kr_pack = jnp.concatenate([w_kr, _rot_cols(w_kr, MLA_ROPE),
                               jnp.zeros((depth, dm, LANE - 2 * MLA_ROPE), F32)], axis=-1)
    w_ext = jnp.concatenate([col(0), col(1), col(2), kr_pack, col(4), _rot_cols(col(4), DIFF_HD),
                             col(5), _rot_cols(col(5), DIFF_HD), col(6), col(7)], axis=-1).astype(BF16)
    e_sizes = (ssm_w, q_lora, kv_lora, LANE, dq_w, dq_w, dk_w, dk_w, dv_w, n_br * dm)
    e_offs = [0]
    for s in e_sizes:
        e_offs.append(e_offs[-1] + s)

    step = jnp.exp(ssm_log_dt)[..., None]
    mag = jnp.exp(ssm_lambda_re * step)
    a_re = mag * jnp.cos(ssm_lambda_im * step)
    a_im = mag * jnp.sin(ssm_lambda_im * step)
    den = ssm_lambda_re * ssm_lambda_re + ssm_lambda_im * ssm_lambda_im
    z_re = ((a_re - 1.0) * ssm_lambda_re + a_im * ssm_lambda_im) / den
    z_im = (a_im * ssm_lambda_re - (a_re - 1.0) * ssm_lambda_im) / den
    bb_re = z_re[..., None] * ssm_b_re - z_im[..., None] * ssm_b_im
    bb_im = z_re[..., None] * ssm_b_im + z_im[..., None] * ssm_b_re
    bd = jax.vmap(_block_diag)
    wb = jnp.concatenate([bd(jnp.swapaxes(bb_re, -1, -2)), bd(jnp.swapaxes(bb_im, -1, -2))],
                         axis=-1).astype(BF16)
    wc = jnp.concatenate([bd(jnp.swapaxes(ssm_c_re, -1, -2)), -bd(jnp.swapaxes(ssm_c_im, -1, -2))],
                         axis=-2).astype(BF16)
    a_vec = jnp.stack([a_re.reshape(depth, ns), a_im.reshape(depth, ns)], axis=1).reshape(depth, 1, 2 * ns)
    p_re, p_im = _cplx_pow2(a_re, a_im, int(math.log2(chunk_len)))
    a_pow = jnp.stack([p_re.reshape(depth, ns), p_im.reshape(depth, ns)], axis=1).reshape(depth, 1, 2 * ns)
    d_vec = ssm_d.reshape(depth, 1, ssm_w)
    w_glu = ssm_w_glu.astype(BF16)

    w_uq_nope = mla_w_uq.reshape(depth, q_lora, MLA_HEADS, mla_qd)[..., :MLA_NOPE].reshape(depth, q_lora, -1)
    w_uq_rope = mla_w_uq.reshape(depth, q_lora, MLA_HEADS, mla_qd)[..., MLA_NOPE:].reshape(depth, q_lora, -1)
    w_uq_ext = jnp.concatenate([w_uq_nope, w_uq_rope, _rot_cols(w_uq_rope, MLA_ROPE)], axis=-1).astype(BF16)
    w_uk_bd = bd(jnp.transpose(mla_w_uk, (0, 2, 3, 1))).astype(BF16)
    w_uv_bd = bd(jnp.transpose(mla_w_uv, (0, 2, 1, 3))).astype(BF16)

    lam = (jnp.exp(jnp.sum(diff_lambda_q1 * diff_lambda_k1, axis=-1))
           - jnp.exp(jnp.sum(diff_lambda_q2 * diff_lambda_k2, axis=-1)))
    w_bs, w_bm, w_bd_, w_o = (w.astype(BF16) for w in (w_br_ssm, w_br_mla, w_br_diff, w_out))
    w_u, w_dn = w_up.astype(BF16), w_down.astype(BF16)
    bg = b_gate.reshape(depth, 1, n_br * dm)

    pos = jnp.concatenate([jnp.tile(jnp.arange(seq), nbp), jnp.tile(past_len + jnp.arange(steps), nbs)])
    cos32, sin32 = _rope_tables(pos, MLA_ROPE)
    cos64, sin64 = _rope_tables(pos, DIFF_HD)

    cache_krt = jnp.transpose(cache_mla_krope, (0, 1, 3, 2))
    cache_dkt = jnp.transpose(cache_diff_k, (0, 1, 3, 4, 5, 2)).reshape(depth, n_pool, dk_w, page)
    cache_dv2 = cache_diff_v.reshape(depth, n_pool, page * DIFF_KV_HEADS, DIFF_VD)
    s0_grp = jnp.concatenate([state_ssm_re.reshape(depth, nbs, ns), state_ssm_im.reshape(depth, nbs, ns)],
                             axis=-1).reshape(depth, nbs // SUBLANE, SUBLANE, 2 * ns)

    x = jnp.concatenate([x_prompt.reshape(tp, dm), x_sample.reshape(ts, dm)], axis=0)
    rows = [[] for _ in range(12)]
    for l in range(depth):
        z = _norm_mm(x, g_mix[l][None], w_ext[l], tm, 2)
        zc = lambda i: z[:, e_offs[i]:e_offs[i + 1]]
        u, cq, ckv, krp, dq, dq_r, dk, dk_r, dv, gz = (zc(i) for i in range(10))

        u_p = u[:tp].reshape(nbp, seq // tl, SUBLANE, chunk_len, ssm_w)
        u_p = jnp.swapaxes(u_p, 2, 3).reshape(nbp, seq, ssm_w)
        y_p, fin_p = _ssm_prompt(u_p, wb[l], a_vec[l], a_pow[l], wc[l], d_vec[l], tl)
        y_p = jnp.swapaxes(y_p.reshape(nbp, seq // tl, chunk_len, SUBLANE, ssm_w), 2, 3).reshape(tp, ssm_w)
        u_s = u[tp:].reshape(nbs // SUBLANE, SUBLANE, steps, ssm_w)
        u_s = jnp.swapaxes(u_s, 1, 2).reshape(nbs // SUBLANE, steps * SUBLANE, ssm_w)
        y_s, fin_s = _ssm_sample(u_s, s0_grp[l], wb[l], a_vec[l], wc[l], d_vec[l], steps)
        y_s = jnp.swapaxes(y_s.reshape(nbs // SUBLANE, steps, SUBLANE, ssm_w), 1, 2).reshape(ts, ssm_w)
        y = jax.nn.gelu(jnp.concatenate([y_p, y_s], axis=0))
        y_ssm = y * jax.nn.sigmoid(_mm(y, w_glu[l], tm) + ssm_b_glu[l])
        fin_p = fin_p.reshape(nbp, 2, n_grp, SSM_STATE)
        fin_s = fin_s.reshape(nbs, 2, n_grp, SSM_STATE)

        q = _mm(_rms(cq, mla_g_q[l]), w_uq_ext[l], tm)
        n_nope = MLA_HEADS * MLA_NOPE
        n_rope = MLA_HEADS * MLA_ROPE
        q_lat = _mm(q[:, :n_nope], w_uk_bd[l], tm)
        q_rope = _apply_rope(q[:, n_nope:n_nope + n_rope], q[:, n_nope + n_rope:], cos32, sin32, MLA_ROPE)
        q_lat = (q_lat * (mla_scale * LOG2E)).astype(BF16).reshape(tt, MLA_HEADS, kv_lora)
        q_rope = (q_rope * (mla_scale * LOG2E)).astype(BF16).reshape(tt, MLA_HEADS, MLA_ROPE)
        ckv_n = _rms(ckv, mla_g_kv[l])
        kr = krp[:, :MLA_ROPE] * cos32 + krp[:, MLA_ROPE:2 * MLA_ROPE] * sin32
        dq_pad_w = kv_lora + LANE
        q_pad = jnp.concatenate([q_lat[:tp], q_rope[:tp], jnp.zeros((tp, MLA_HEADS, LANE - MLA_ROPE), BF16)], axis=-1)
        k_pad = jnp.concatenate([ckv_n[:tp], kr[:tp], jnp.zeros((tp, LANE - MLA_ROPE), F32)], axis=-1).astype(BF16)

        qp = jnp.transpose(q_pad.reshape(nbp, seq, MLA_HEADS, dq_pad_w), (0, 2, 1, 3))
        ktp = jnp.swapaxes(k_pad.reshape(nbp, seq // tk, tk, dq_pad_w), 2, 3)
        vp = ckv_n[:tp].astype(BF16).reshape(nbp, seq // tk, tk, kv_lora)
        o_p = _flash(qp, ktp, vp, tq)
        o_p = jnp.transpose(o_p, (0, 2, 1, 3)).reshape(tp, MLA_HEADS * kv_lora)

        qd = _apply_rope(dq, dq_r, cos64, sin64, DIFF_HD) * (diff_scale * LOG2E)
        kd = _apply_rope(dk, dk_r, cos64, sin64, DIFF_HD)
        qd6 = qd.reshape(tt, DIFF_KV_HEADS, DIFF_GROUP, 2, DIFF_HD)
        zeros = jnp.zeros_like(qd6[..., 0, :])
        qd_blk = jnp.stack([jnp.concatenate([qd6[..., 0, :], zeros], axis=-1),
                            jnp.concatenate([zeros, qd6[..., 1, :]], axis=-1)], axis=-2).astype(BF16)
        qdp = jnp.transpose(qd_blk[:tp].reshape(nbp, seq, DIFF_KV_HEADS, DIFF_GROUP * 2, 2 * DIFF_HD),
                            (0, 2, 3, 1, 4)).reshape(nbp * DIFF_KV_HEADS, DIFF_GROUP * 2, seq, 2 * DIFF_HD)
        kdp = jnp.transpose(kd[:tp].astype(BF16).reshape(nbp, seq // tk, tk, DIFF_KV_HEADS, 2 * DIFF_HD),
                            (0, 3, 1, 4, 2)).reshape(nbp * DIFF_KV_HEADS, seq // tk, 2 * DIFF_HD, tk)
        vdp = jnp.transpose(dv[:tp].astype(BF16).reshape(nbp, seq // tk, tk, DIFF_KV_HEADS, DIFF_VD),
                            (0, 3, 1, 2, 4)).reshape(nbp * DIFF_KV_HEADS, seq // tk, tk, DIFF_VD)
        od_p = _flash(qdp, kdp, vdp, tq)
        od_p = jnp.transpose(od_p.reshape(nbp, DIFF_KV_HEADS, DIFF_GROUP, 2, seq, DIFF_VD), (0, 4, 1, 2, 3, 5))
        od_p = od_p.reshape(tp, DIFF_HEADS, 2, DIFF_VD)

        by_tok = lambda a: a[tp:].reshape(nbs, steps, -1)
        pad_rows = lambda a: jnp.pad(a, ((0, 0), (0, page * (a.shape[1] // steps) - a.shape[1]), (0, 0)))
        pad_cols = lambda a: jnp.pad(jnp.swapaxes(a, 1, 2), ((0, 0), (0, 0), (0, page - steps)))
        ql_s = jnp.transpose(q_lat[tp:].reshape(nbs, steps, MLA_HEADS, kv_lora), (0, 2, 1, 3))
        ql_s = ql_s.reshape(nbs, MLA_HEADS * steps, kv_lora)
        qr_s = jnp.transpose(q_rope[tp:].reshape(nbs, steps, MLA_HEADS, MLA_ROPE), (0, 2, 1, 3))
        qr_s = qr_s.reshape(nbs, MLA_HEADS * steps, MLA_ROPE)
        qd_s = jnp.transpose(qd_blk[tp:].reshape(nbs, steps, DIFF_KV_HEADS, DIFF_GROUP * 2, 2 * DIFF_HD),
                             (0, 2, 3, 1, 4)).reshape(nbs, DIFF_KV_HEADS, DIFF_GROUP * 2 * steps, 2 * DIFF_HD)
        new_dv = by_tok(dv).reshape(nbs, steps * DIFF_KV_HEADS, DIFF_VD)
        o_all = _paged(page_table, l, ql_s, qr_s, qd_s, pad_rows(by_tok(ckv_n)), pad_cols(by_tok(kr)),
                       pad_cols(by_tok(kd)), pad_rows(new_dv),
                       cache_mla_latent, cache_krt, cache_dkt, cache_dv2, steps)
        rows_m = MLA_HEADS * steps
        o_s = jnp.transpose(o_all[:, :rows_m].reshape(nbs, MLA_HEADS, steps, kv_lora), (0, 2, 1, 3))
        o_s = o_s.reshape(ts, MLA_HEADS * kv_lora)
        od_s = jnp.transpose(o_all[:, rows_m:].reshape(nbs, DIFF_KV_HEADS, DIFF_GROUP, 2, steps, DIFF_VD),
                             (0, 4, 1, 2, 3, 5))
        od_s = od_s.reshape(ts, DIFF_HEADS, 2, DIFF_VD)

        y_mla = _mm(jnp.concatenate([o_p, o_s], axis=0), w_uv_bd[l], tm)
        lam_init = 0.8 - 0.6 * math.exp(-0.3 * l)
        lam_full = lam[l] + lam_init
        od = jnp.concatenate([od_p, od_s], axis=0)
        od = od[:, :, 0, :] - lam_full * od[:, :, 1, :]
        y_diff = (_rms(od, diff_g_sub[l]) * (1.0 - lam_init)).reshape(tt, DIFF_HEADS * DIFF_VD)

        x = _merge(x, y_ssm, y_mla, y_diff, gz, bg[l], w_bs[l], w_bm[l], w_bd_[l], w_o[l], tm)
        x = _mlp(x, g_mlp[l][None], w_u[l], w_dn[l], tm, 1024)

        new = (ckv_n, kr, kd, dv)
        shapes = ((kv_lora,), (MLA_ROPE,), (DIFF_KV_HEADS, 2, DIFF_HD), (DIFF_KV_HEADS, DIFF_VD))
        for i, (a, shp) in enumerate(zip(new, shapes)):
            rows[i].append(a[:tp].reshape((nbp, seq) + shp))
            rows[6 + i].append(a[tp:].reshape((nbs, steps) + shp))
        rows[4].append(fin_p[:, 0])
        rows[5].append(fin_p[:, 1])
        rows[10].append(fin_s[:, 0])
        rows[11].append(fin_s[:, 1])

    y = _norm(x, g_final[None], tm)
    outs = [jnp.stack(r) for r in rows]
    return (y[:tp].reshape(nbp, seq, dm), y[tp:].reshape(nbs, steps, dm), *outs)
```

```python
import functools
import math

import jax
import jax.numpy as jnp
from jax import lax
from jax.experimental import pallas as pl
from jax.experimental.pallas import tpu as pltpu

F32 = jnp.float32
BF16 = jnp.bfloat16

SSM_GROUP = 16
SSM_STATE = 64
MLA_HEADS = 8
MLA_NOPE = 64
MLA_ROPE = 32
DIFF_HEADS = 4
DIFF_KV_HEADS = 2
DIFF_GROUP = DIFF_HEADS // DIFF_KV_HEADS
DIFF_HD = 64
DIFF_VD = 2 * DIFF_HD
ROPE_THETA = 10000.0
EPS = 1e-6

LANE = 128
SUBLANE = 8
VMEM_LIMIT = 56 * 1024 * 1024


def _cparams(*sem):
    return pltpu.CompilerParams(dimension_semantics=sem, vmem_limit_bytes=VMEM_LIMIT)


def _rms(x, g):
    ms = jnp.mean(x * x, axis=-1, keepdims=True)
    return x * lax.rsqrt(ms + EPS) * g


def _mm_kernel(x_ref, w_ref, o_ref):
    o_ref[...] = jnp.dot(x_ref[...].astype(BF16), w_ref[...],
                         preferred_element_type=F32).astype(o_ref.dtype)


def _mm(x, w, tm, out_dtype=F32):
    t, k = x.shape
    n = w.shape[1]
    return pl.pallas_call(
        _mm_kernel,
        grid=(t // tm,),
        in_specs=[pl.BlockSpec((tm, k), lambda i: (i, 0)),
                  pl.BlockSpec((k, n), lambda i: (0, 0))],
        out_specs=pl.BlockSpec((tm, n), lambda i: (i, 0)),
        out_shape=jax.ShapeDtypeStruct((t, n), out_dtype),
        compiler_params=_cparams("parallel"),
        name="mm",
    )(x, w)


def _in_proj_kernel(x_ref, g_ref, w_ref, gq_ref, gkv_ref, c64_ref, s64_ref, c32_ref, s32_ref,
                    u_ref, cq_ref, ckv_ref, kr_ref, kpad_ref, qd_ref, kd_ref, kdb_ref, dv_ref, dvb_ref, gz_ref,
                    *, offs, q_scale):
    h = _rms(x_ref[...], g_ref[...]).astype(BF16)

    def proj(i):
        return jnp.dot(h, w_ref[:, offs[i]:offs[i + 1]], preferred_element_type=F32)

    u_ref[...] = proj(0)
    cq_ref[...] = _rms(proj(1), gq_ref[...]).astype(BF16)
    ckv = _rms(proj(2), gkv_ref[...])
    ckv_ref[...] = ckv
    kr = proj(3) * c32_ref[...] + proj(4) * s32_ref[...]
    kr_ref[...] = kr
    kpad_ref[...] = jnp.concatenate([ckv, kr], axis=1).astype(BF16)

    c64 = c64_ref[...]
    s64 = s64_ref[...]
    dq = proj(5)
    n_q = dq.shape[1] // LANE
    qd = (dq * jnp.tile(c64, (1, n_q)) + proj(6) * jnp.tile(s64, (1, n_q))) * q_scale
    low = lax.broadcasted_iota(jnp.int32, (dq.shape[0], LANE), 1) < DIFF_HD
    blocks = []
    for j in range(n_q):
        v = qd[:, j * LANE:(j + 1) * LANE]
        blocks += [jnp.where(low, v, 0.0), jnp.where(low, 0.0, v)]
    qd_ref[...] = jnp.concatenate(blocks, axis=1).astype(BF16)
    dk = proj(7)
    n_k = dk.shape[1] // LANE
    kd = dk * jnp.tile(c64, (1, n_k)) + proj(8) * jnp.tile(s64, (1, n_k))
    kd_ref[...] = kd
    kdb_ref[...] = kd.astype(BF16)
    dv = proj(9)
    dv_ref[...] = dv
    dvb_ref[...] = dv.astype(BF16)
    gz_ref[...] = proj(10)


def _in_proj(x, g, w, g_q, g_kv, c64, s64, c32, s32, offs, q_scale, tm):
    t, k = x.shape
    widths = [offs[i + 1] - offs[i] for i in range(len(offs) - 1)]
    row = lambda i: (i, 0)
    const = lambda i: (0, 0)
    outs = [(widths[0], F32), (widths[1], BF16), (widths[2], F32), (LANE, F32), (widths[2] + LANE, BF16),
            (2 * widths[5], BF16), (widths[7], F32), (widths[7], BF16), (widths[9], F32), (widths[9], BF16),
            (widths[10], F32)]
    return pl.pallas_call(
        functools.partial(_in_proj_kernel, offs=tuple(offs), q_scale=q_scale),
        grid=(t // tm,),
        in_specs=[pl.BlockSpec((tm, k), row), pl.BlockSpec((1, k), const), pl.BlockSpec(w.shape, const),
                  pl.BlockSpec(g_q.shape, const), pl.BlockSpec(g_kv.shape, const)]
                 + [pl.BlockSpec((tm, LANE), row)] * 4,
        out_specs=[pl.BlockSpec((tm, n), row) for n, _ in outs],
        out_shape=[jax.ShapeDtypeStruct((t, n), dt) for n, dt in outs],
        compiler_params=_cparams("parallel"),
        name="in_proj",
    )(x, g, w, g_q, g_kv, c64, s64, c32, s32)


def _q_proj_kernel(cq_ref, wq_ref, wuk_ref, c32_ref, s32_ref, o_ref, *, n_nope, n_heads, q_scale):
    q = jnp.dot(cq_ref[...], wq_ref[...], preferred_element_type=F32)
    q_lat = jnp.dot(q[:, :n_nope].astype(BF16), wuk_ref[...], preferred_element_type=F32) * q_scale
    n_r = n_heads * LANE
    rope = (q[:, n_nope:n_nope + n_r] * jnp.tile(c32_ref[...], (1, n_heads))
            + q[:, n_nope + n_r:] * jnp.tile(s32_ref[...], (1, n_heads))) * q_scale
    blocks = []
    for hd in range(n_heads):
        blocks += [q_lat[:, hd * LANE:(hd + 1) * LANE], rope[:, hd * LANE:(hd + 1) * LANE]]
    o_ref[...] = jnp.concatenate(blocks, axis=1).astype(BF16)


def _q_proj(cq, wq, wuk, c32, s32, n_nope, n_heads, q_scale, tm):
    t, k = cq.shape
    row = lambda i: (i, 0)
    const = lambda i: (0, 0)
    n_out = n_heads * 2 * LANE
    return pl.pallas_call(
        functools.partial(_q_proj_kernel, n_nope=n_nope, n_heads=n_heads, q_scale=q_scale),
        grid=(t // tm,),
        in_specs=[pl.BlockSpec((tm, k), row), pl.BlockSpec(wq.shape, const), pl.BlockSpec(wuk.shape, const),
                  pl.BlockSpec((tm, LANE), row), pl.BlockSpec((tm, LANE), row)],
        out_specs=pl.BlockSpec((tm, n_out), row),
        out_shape=jax.ShapeDtypeStruct((t, n_out), BF16),
        compiler_params=_cparams("parallel"),
        name="q_proj",
    )(cq, wq, wuk, c32, s32)


def _merge_kernel(x_ref, ys_ref, om_ref, yd_ref, gz_ref, bg_ref, wuv_ref, ws_ref, wm_ref, wd_ref, wo_ref, o_ref):
    d = x_ref.shape[1]
    y_mla = jnp.dot(om_ref[...].astype(BF16), wuv_ref[...], preferred_element_type=F32)
    merged = None
    for b, (y, w_ref) in enumerate(((ys_ref[...], ws_ref), (y_mla, wm_ref), (yd_ref[...], wd_ref))):
        gate = jax.nn.sigmoid(gz_ref[:, b * d:(b + 1) * d] + bg_ref[:, b * d:(b + 1) * d])
        term = gate * jnp.dot(y.astype(BF16), w_ref[...], preferred_element_type=F32)
        merged = term if merged is None else merged + term
    o_ref[...] = x_ref[...] + jnp.dot(merged.astype(BF16), wo_ref[...], preferred_element_type=F32)


def _merge(x, y_ssm, y_mla, y_diff, gz, b_gate, w_uv, w_s, w_m, w_d, w_o, tm):
    t, d = x.shape
    row = lambda i: (i, 0)
    const = lambda i: (0, 0)
    return pl.pallas_call(
        _merge_kernel,
        grid=(t // tm,),
        in_specs=[pl.BlockSpec((tm, d), row),
                  pl.BlockSpec((tm, y_ssm.shape[1]), row),
                  pl.BlockSpec((tm, y_mla.shape[1]), row),
                  pl.BlockSpec((tm, y_diff.shape[1]), row),
                  pl.BlockSpec((tm, gz.shape[1]), row),
                  pl.BlockSpec(b_gate.shape, const),
                  pl.BlockSpec(w_uv.shape, const),
                  pl.BlockSpec(w_s.shape, const),
                  pl.BlockSpec(w_m.shape, const),
                  pl.BlockSpec(w_d.shape, const),
                  pl.BlockSpec(w_o.shape, const)],
        out_specs=pl.BlockSpec((tm, d), row),
        out_shape=jax.ShapeDtypeStruct((t, d), F32),
        compiler_params=_cparams("parallel"),
        name="merge",
    )(x, y_ssm, y_mla, y_diff, gz, b_gate, w_uv, w_s, w_m, w_d, w_o)


def _mlp_kernel(x_ref, g_ref, wu_ref, wd_ref, o_ref, h_sc):
    f = pl.program_id(1)

    @pl.when(f == 0)
    def _():
        x = x_ref[...]
        h_sc[...] = _rms(x, g_ref[...]).astype(BF16)
        o_ref[...] = x

    a = jnp.dot(h_sc[...], wu_ref[...], preferred_element_type=F32)
    a = jnp.square(jnp.maximum(a, 0.0)).astype(BF16)
    o_ref[...] += jnp.dot(a, wd_ref[...], preferred_element_type=F32)


def _mlp(x, g, w_up, w_down, tm, tf):
    t, d = x.shape
    ff = w_up.shape[1]
    return pl.pallas_call(
        _mlp_kernel,
        grid=(t // tm, ff // tf),
        in_specs=[pl.BlockSpec((tm, d), lambda i, f: (i, 0)),
                  pl.BlockSpec((1, d), lambda i, f: (0, 0)),
                  pl.BlockSpec((d, tf), lambda i, f: (0, f)),
                  pl.BlockSpec((tf, d), lambda i, f: (f, 0))],
        out_specs=pl.BlockSpec((tm, d), lambda i, f: (i, 0)),
        out_shape=jax.ShapeDtypeStruct((t, d), F32),
        scratch_shapes=[pltpu.VMEM((tm, d), BF16)],
        compiler_params=_cparams("parallel", "arbitrary"),
        name="mlp",
    )(x, g, w_up, w_down)


def _norm_kernel(x_ref, g_ref, o_ref):
    o_ref[...] = _rms(x_ref[...], g_ref[...])


def _norm(x, g, tm):
    t, d = x.shape
    return pl.pallas_call(
        _norm_kernel,
        grid=(t // tm,),
        in_specs=[pl.BlockSpec((tm, d), lambda i: (i, 0)), pl.BlockSpec((1, d), lambda i: (0, 0))],
        out_specs=pl.BlockSpec((tm, d), lambda i: (i, 0)),
        out_shape=jax.ShapeDtypeStruct((t, d), F32),
        compiler_params=_cparams("parallel"),
        name="final_norm",
    )(x, g)


SCAN_COLS = 4 * LANE


def _cplx_step(ar, ai, sr, si, xr, xi):
    return ar * sr - ai * si + xr, ar * si + ai * sr + xi


def _ssm_prompt_kernel(u_ref, wb_ref, a_ref, ap_ref, wc_ref, d_ref, y_ref, fin_ref,
                       xs, state_sc, carry_sc, fin_sc, *, chunk_len, ns):
    li = pl.program_id(1)

    @pl.when(li == 0)
    def _():
        state_sc[...] = jnp.zeros_like(state_sc)

    u = u_ref[0]
    xs[...] = jnp.dot(u.astype(BF16), wb_ref[...], preferred_element_type=F32)

    n_cb = ns // SCAN_COLS
    for cb in range(n_cb):
        re = pl.ds(cb * SCAN_COLS, SCAN_COLS)
        im = pl.ds(ns + cb * SCAN_COLS, SCAN_COLS)
        ar = jnp.broadcast_to(a_ref[0:1, re], (SUBLANE, SCAN_COLS))
        ai = jnp.broadcast_to(a_ref[0:1, im], (SUBLANE, SCAN_COLS))

        def sweep1(t, s):
            rows = pl.ds(pl.multiple_of(t * SUBLANE, SUBLANE), SUBLANE)
            return _cplx_step(ar, ai, s[0], s[1], xs[rows, re], xs[rows, im])

        zero = jnp.zeros((SUBLANE, SCAN_COLS), F32)
        fr, fi = lax.fori_loop(0, chunk_len, sweep1, (zero, zero))
        fin_sc[:, re] = fr
        fin_sc[:, im] = fi

    carry_sc[0:1, :] = state_sc[0:1, :]
    pr = ap_ref[0:1, 0:ns]
    pi = ap_ref[0:1, ns:2 * ns]
    for c in range(1, SUBLANE):
        cr = carry_sc[c - 1:c, 0:ns]
        ci = carry_sc[c - 1:c, ns:2 * ns]
        nr, ni = _cplx_step(pr, pi, cr, ci, fin_sc[c - 1:c, 0:ns], fin_sc[c - 1:c, ns:2 * ns])
        carry_sc[c:c + 1, 0:ns] = nr
        carry_sc[c:c + 1, ns:2 * ns] = ni

    for cb in range(n_cb):
        re = pl.ds(cb * SCAN_COLS, SCAN_COLS)
        im = pl.ds(ns + cb * SCAN_COLS, SCAN_COLS)
        ar = jnp.broadcast_to(a_ref[0:1, re], (SUBLANE, SCAN_COLS))
        ai = jnp.broadcast_to(a_ref[0:1, im], (SUBLANE, SCAN_COLS))

        def sweep2(t, s):
            rows = pl.ds(pl.multiple_of(t * SUBLANE, SUBLANE), SUBLANE)
            nr, ni = _cplx_step(ar, ai, s[0], s[1], xs[rows, re], xs[rows, im])
            xs[rows, re] = nr
            xs[rows, im] = ni
            return nr, ni

        fr, fi = lax.fori_loop(0, chunk_len, sweep2, (carry_sc[:, re], carry_sc[:, im]))
        state_sc[0:1, re] = fr[SUBLANE - 1:SUBLANE]
        state_sc[0:1, im] = fi[SUBLANE - 1:SUBLANE]

    y_ref[0] = jnp.dot(xs[...].astype(BF16), wc_ref[...], preferred_element_type=F32) + d_ref[...] * u
    fin_ref[0] = state_sc[0:1, :]


def _ssm_prompt(u_perm, wb, a_vec, a_pow, wc, d_vec, tl):
    nb, seq, w = u_perm.shape
    ns2 = wb.shape[1]
    const = lambda b, i: (0, 0)
    kern = functools.partial(_ssm_prompt_kernel, chunk_len=tl // SUBLANE, ns=ns2 // 2)
    return pl.pallas_call(
        kern,
        grid=(nb, seq // tl),
        in_specs=[pl.BlockSpec((1, tl, w), lambda b, i: (b, i, 0)),
                  pl.BlockSpec(wb.shape, const),
                  pl.BlockSpec(a_vec.shape, const),
                  pl.BlockSpec(a_pow.shape, const),
                  pl.BlockSpec(wc.shape, const),
                  pl.BlockSpec(d_vec.shape, const)],
        out_specs=[pl.BlockSpec((1, tl, w), lambda b, i: (b, i, 0)),
                   pl.BlockSpec((1, 1, ns2), lambda b, i: (b, 0, 0))],
        out_shape=[jax.ShapeDtypeStruct((nb, seq, w), F32),
                   jax.ShapeDtypeStruct((nb, 1, ns2), F32)],
        scratch_shapes=[pltpu.VMEM((tl, ns2), F32),
                        pltpu.VMEM((SUBLANE, ns2), F32),
                        pltpu.VMEM((SUBLANE, ns2), F32),
                        pltpu.VMEM((SUBLANE, ns2), F32)],
        compiler_params=_cparams("parallel", "arbitrary"),
        name="ssm_prompt",
    )(u_perm, wb, a_vec, a_pow, wc, d_vec)


def _ssm_sample_kernel(u_ref, s0_ref, wb_ref, a_ref, wc_ref, d_ref, y_ref, fin_ref, xs, *, steps, ns):
    u = u_ref[0]
    xs[...] = jnp.dot(u.astype(BF16), wb_ref[...], preferred_element_type=F32)
    for cb in range(ns // SCAN_COLS):
        re = pl.ds(cb * SCAN_COLS, SCAN_COLS)
        im = pl.ds(ns + cb * SCAN_COLS, SCAN_COLS)
        ar = jnp.broadcast_to(a_ref[0:1, re], (SUBLANE, SCAN_COLS))
        ai = jnp.broadcast_to(a_ref[0:1, im], (SUBLANE, SCAN_COLS))
        sr = s0_ref[0, :, re]
        si = s0_ref[0, :, im]
        for t in range(steps):
            rows = pl.ds(t * SUBLANE, SUBLANE)
            sr, si = _cplx_step(ar, ai, sr, si, xs[rows, re], xs[rows, im])
            xs[rows, re] = sr
            xs[rows, im] = si
        fin_ref[0, :, re] = sr
        fin_ref[0, :, im] = si
    y_ref[0] = jnp.dot(xs[...].astype(BF16), wc_ref[...], preferred_element_type=F32) + d_ref[...] * u


def _ssm_sample(u_grp, s0_grp, wb, a_vec, wc, d_vec, steps):
    ng, rows, w = u_grp.shape
    ns2 = wb.shape[1]
    const = lambda g: (0, 0)
    kern = functools.partial(_ssm_sample_kernel, steps=steps, ns=ns2 // 2)
    return pl.pallas_call(
        kern,
        grid=(ng,),
        in_specs=[pl.BlockSpec((1, rows, w), lambda g: (g, 0, 0)),
                  pl.BlockSpec((1, SUBLANE, ns2), lambda g: (g, 0, 0)),
                  pl.BlockSpec(wb.shape, const),
                  pl.BlockSpec(a_vec.shape, const),
                  pl.BlockSpec(wc.shape, const),
                  pl.BlockSpec(d_vec.shape, const)],
        out_specs=[pl.BlockSpec((1, rows, w), lambda g: (g, 0, 0)),
                   pl.BlockSpec((1, SUBLANE, ns2), lambda g: (g, 0, 0))],
        out_shape=[jax.ShapeDtypeStruct((ng, rows, w), F32),
                   jax.ShapeDtypeStruct((ng, SUBLANE, ns2), F32)],
        scratch_shapes=[pltpu.VMEM((rows, ns2), F32)],
        compiler_params=_cparams("parallel"),
        name="ssm_sample",
    )(u_grp, s0_grp, wb, a_vec, wc, d_vec)


def _nt_dot(a, b):
    return lax.dot_general(a, b, (((1,), (1,)), ((), ())), preferred_element_type=F32)


def _softmax_step(s, v, m_sc, l_sc, acc_sc):
    reps = s.shape[1] // LANE
    m_prev = m_sc[...]
    m_next = jnp.maximum(m_prev, jnp.max(s, axis=1, keepdims=True))
    p = jnp.exp2(s - jnp.tile(m_next, (1, reps)))
    alpha = jnp.exp2(m_prev - m_next)
    l_sc[...] = alpha * l_sc[...] + jnp.sum(p, axis=1, keepdims=True)
    acc_sc[...] = alpha * acc_sc[...] + jnp.dot(p.astype(BF16), v, preferred_element_type=F32)
    m_sc[...] = m_next


def _flash_kernel(q_ref, k_ref, v_ref, par_ref, o_ref, m_sc, l_sc, acc_sc, *, n_stack, tk, diff):
    tq = q_ref.shape[0]
    d = k_ref.shape[1]
    i = pl.program_id(1)
    q = jnp.concatenate([q_ref[:, s * d:(s + 1) * d] for s in range(n_stack)], axis=0)
    m_sc[...] = jnp.full_like(m_sc, -jnp.inf)
    l_sc[...] = jnp.zeros_like(l_sc)
    acc_sc[...] = jnp.zeros_like(acc_sc)
    n_full = (i * tq) // tk
    first_key = n_full * tk

    def chunk(j):
        keys = pl.ds(pl.multiple_of(j * tk, tk), tk)
        return _nt_dot(q, k_ref[keys, :]), v_ref[keys, :]

    def body(j, carry):
        s, v = chunk(j)
        _softmax_step(s, v, m_sc, l_sc, acc_sc)
        return carry

    lax.fori_loop(0, n_full, body, 0)

    s, v = chunk(n_full)
    row = lax.broadcasted_iota(jnp.int32, (tq, tk), 0) + (i * tq - first_key)
    col = lax.broadcasted_iota(jnp.int32, (tq, tk), 1)
    s = jnp.where((col <= row)[None], s.reshape(n_stack, tq, tk), -jnp.inf).reshape(n_stack * tq, tk)
    _softmax_step(s, v, m_sc, l_sc, acc_sc)
    o = acc_sc[...] / l_sc[...]
    parts = [o[s * tq:(s + 1) * tq] for s in range(n_stack)]
    if diff:
        lam, gain, post = par_ref[0:1, :], par_ref[1:2, :], par_ref[2:3, :]
        parts = [_rms(parts[2 * g] - lam * parts[2 * g + 1], gain) * post for g in range(n_stack // 2)]
    o_ref[...] = jnp.concatenate(parts, axis=1)


def _flash(q, k, v, par, n_seq, seq, n_stack, n_col, tq, tk, diff):
    d = q.shape[1] // (n_col * n_stack)
    n_q = seq // tq
    out_w = n_stack * LANE // (2 if diff else 1)
    return pl.pallas_call(
        functools.partial(_flash_kernel, n_stack=n_stack, tk=tk, diff=diff),
        grid=(n_seq * n_col, n_q),
        in_specs=[pl.BlockSpec((tq, n_stack * d), lambda b, i: ((b // n_col) * n_q + i, b % n_col)),
                  pl.BlockSpec((seq, d), lambda b, i: (b // n_col, b % n_col)),
                  pl.BlockSpec((seq, LANE), lambda b, i: (b // n_col, b % n_col)),
                  pl.BlockSpec(par.shape, lambda b, i: (0, 0))],
        out_specs=pl.BlockSpec((tq, out_w), lambda b, i: ((b // n_col) * n_q + i, b % n_col)),
        out_shape=jax.ShapeDtypeStruct((n_seq * seq, n_col * out_w), F32),
        scratch_shapes=[pltpu.VMEM((n_stack * tq, LANE), F32)] * 3,
        compiler_params=_cparams("parallel", "parallel"),
        name="flash",
    )(q, k, v, par)


PAGES_PER_STEP = 16
LOG2E = math.log2(math.e)


def _paged_kernel(pt_ref, ql_ref, qr_ref, qd_ref, nlat_ref, nkr_ref, ndk_ref, ndv_ref, *rest,
                  n_pages, steps):
    pages = rest[:4 * n_pages]
    o_ref, m_sc, l_sc, acc_sc = rest[4 * n_pages:]
    g = pl.program_id(1)
    n_kv = qd_ref.shape[1]
    rows_m = ql_ref.shape[1]
    rows_d = qd_ref.shape[2]
    page = nlat_ref.shape[1]
    hd2 = qd_ref.shape[3]

    @pl.when(g == 0)
    def _():
        m_sc[...] = jnp.full_like(m_sc, -jnp.inf)
        l_sc[...] = jnp.zeros_like(l_sc)
        acc_sc[...] = jnp.zeros_like(acc_sc)

    def attend(lat_refs, kr_refs, dk_refs, dv_refs, mask):
        cat = lambda xs, axis: xs[0] if len(xs) == 1 else jnp.concatenate(xs, axis=axis)
        lat = cat([r[...].astype(BF16) for r in lat_refs], 0)
        krt = cat([r[...].astype(BF16) for r in kr_refs], 1)
        scores = [_nt_dot(ql_ref[0], lat) + jnp.dot(qr_ref[0], krt, preferred_element_type=F32)]
        values = [lat]
        for kv in range(n_kv):
            dkt = cat([r[kv * hd2:(kv + 1) * hd2, :].astype(BF16) for r in dk_refs], 1)
            scores.append(jnp.dot(qd_ref[0, kv], dkt, preferred_element_type=F32))
            values.append(cat([r[pl.ds(kv, page, stride=n_kv), :].astype(BF16) for r in dv_refs], 0))
        s = jnp.concatenate(scores, axis=0)
        if mask:
            row = lax.broadcasted_iota(jnp.int32, s.shape, 0) % steps
            col = lax.broadcasted_iota(jnp.int32, s.shape, 1)
            s = jnp.where(col <= row, s, -jnp.inf)
        reps = s.shape[1] // LANE
        m_prev = m_sc[...]
        m_next = jnp.maximum(m_prev, jnp.max(s, axis=1, keepdims=True))
        p = jnp.exp2(s - jnp.tile(m_next, (1, reps)))
        alpha = jnp.exp2(m_prev - m_next)
        l_sc[...] = alpha * l_sc[...] + jnp.sum(p, axis=1, keepdims=True)
        p = p.astype(BF16)
        bounds = [0, rows_m] + [rows_m + (kv + 1) * rows_d for kv in range(n_kv)]
        pv = [jnp.dot(p[lo:hi], v, preferred_element_type=F32)
              for lo, hi, v in zip(bounds[:-1], bounds[1:], values)]
        acc_sc[...] = alpha * acc_sc[...] + jnp.concatenate(pv, axis=0)
        m_sc[...] = m_next

    attend(pages[0:n_pages], pages[n_pages:2 * n_pages], pages[2 * n_pages:3 * n_pages],
           pages[3 * n_pages:4 * n_pages], False)

    @pl.when(g == pl.num_programs(1) - 1)
    def _():
        attend([nlat_ref.at[0]], [nkr_ref.at[0]], [ndk_ref.at[0]], [ndv_ref.at[0]], True)
        o_ref[0] = acc_sc[...] / l_sc[...]


def _paged(page_table, layer, q_lat, q_rope, q_diff, new_lat, new_krt, new_dkt, new_dv,
           cache_lat, cache_krt, cache_dkt, cache_dv, steps):
    nb, n_pages_total = page_table.shape
    npg = math.gcd(PAGES_PER_STEP, n_pages_total)
    n_kv = q_diff.shape[1]
    rows = q_lat.shape[1] + n_kv * q_diff.shape[2]

    def page_spec(cache, p):
        return pl.BlockSpec((None, None) + cache.shape[2:],
                            lambda b, g, pt: (layer, pt[b, g * npg + p], 0, 0))

    per_b3 = lambda b, g, pt: (b, 0, 0)
    per_b4 = lambda b, g, pt: (b, 0, 0, 0)
    operands = [q_lat, q_rope, q_diff, new_lat, new_krt, new_dkt, new_dv]
    in_specs = [pl.BlockSpec((1,) + a.shape[1:], per_b4 if a.ndim == 4 else per_b3) for a in operands]
    for cache in (cache_lat, cache_krt, cache_dkt, cache_dv):
        for p in range(npg):
            in_specs.append(page_spec(cache, p))
            operands.append(cache)
    kern = functools.partial(_paged_kernel, n_pages=npg, steps=steps)
    grid_spec = pltpu.PrefetchScalarGridSpec(
        num_scalar_prefetch=1,
        grid=(nb, n_pages_total // npg),
        in_specs=in_specs,
        out_specs=pl.BlockSpec((1, rows, LANE), per_b3),
        scratch_shapes=[pltpu.VMEM((rows, LANE), F32)] * 3,
    )
    return pl.pallas_call(
        kern,
        grid_spec=grid_spec,
        out_shape=jax.ShapeDtypeStruct((nb, rows, LANE), F32),
        compiler_params=_cparams("parallel", "arbitrary"),
        name="paged",
    )(page_table, *operands)


def _rot_cols(w, seg):
    k, n = w.shape[-2:]
    w4 = w.reshape(w.shape[:-1] + (n // seg, 2, seg // 2))
    return jnp.stack([-w4[..., 1, :], w4[..., 0, :]], axis=-2).reshape(w.shape)


def _rope_tables(pos, d):
    half = d // 2
    inv = ROPE_THETA ** (-jnp.arange(half, dtype=F32) * 2.0 / d)
    ang = pos.astype(F32)[:, None] * inv[None, :]
    cos = jnp.cos(ang)
    sin = jnp.sin(ang)
    return jnp.concatenate([cos, cos], axis=-1), jnp.concatenate([sin, sin], axis=-1)


def _block_diag(blocks):
    g, r, c = blocks.shape
    eye = jnp.eye(g, dtype=blocks.dtype)
    return jnp.einsum('grc,gh->grhc', blocks, eye).reshape(g * r, g * c)


def _cplx_pow2(ar, ai, n_sq):
    for _ in range(n_sq):
        ar, ai = ar * ar - ai * ai, 2.0 * ar * ai
    return ar, ai


def kernel(x_prompt, x_sample, cache_mla_latent, cache_mla_krope, cache_diff_k, cache_diff_v, state_ssm_re, state_ssm_im, page_table, g_mix, w_in, b_gate, ssm_lambda_re, ssm_lambda_im, ssm_log_dt, ssm_b_re, ssm_b_im, ssm_c_re, ssm_c_im, ssm_d, ssm_w_glu, ssm_b_glu, mla_g_q, mla_w_uq, mla_g_kv, mla_w_uk, mla_w_uv, diff_lambda_q1, diff_lambda_k1, diff_lambda_q2, diff_lambda_k2, diff_g_sub, w_br_ssm, w_br_mla, w_br_diff, w_out, g_mlp, w_up, w_down, g_final):
    nbp, seq, dm = x_prompt.shape
    nbs, steps, _ = x_sample.shape
    depth = w_in.shape[0]
    n_pool, page = cache_mla_latent.shape[1:3]
    past_len = page_table.shape[1] * page
    ssm_w = ssm_w_glu.shape[1]
    n_grp = ssm_w // SSM_GROUP
    ns = n_grp * SSM_STATE
    q_lora = mla_g_q.shape[1]
    kv_lora = mla_g_kv.shape[1]
    mla_qd = MLA_NOPE + MLA_ROPE
    mla_scale = 1.0 / math.sqrt(mla_qd)
    diff_scale = 1.0 / math.sqrt(DIFF_HD)
    dq_w = DIFF_HEADS * 2 * DIFF_HD
    dk_w = DIFF_KV_HEADS * 2 * DIFF_HD
    dv_w = DIFF_KV_HEADS * DIFF_VD
    n_br = b_gate.shape[1]
    tp = nbp * seq
    ts = nbs * steps
    tt = tp + ts

    tm = 768 if tt % 768 == 0 else 256
    tm_in = 256
    tq = min(256, seq)
    tk = min(512, seq)
    tl = min(512, seq)
    chunk_len = tl // SUBLANE

    sizes = (ssm_w, q_lora, kv_lora, MLA_ROPE, dq_w, dk_w, dv_w, n_br * dm)
    offs = [0]
    for s in sizes:
        offs.append(offs[-1] + s)
    col = lambda i: w_in[:, :, offs[i]:offs[i + 1]]
    w_kr = col(3)
    lane_pad = lambda w: jnp.concatenate([w, jnp.zeros(w.shape[:-1] + (LANE - w.shape[-1],), F32)], axis=-1)
    w_ext = jnp.concatenate([col(0), col(1), col(2), lane_pad(w_kr), lane_pad(_rot_cols(w_kr, MLA_ROPE)),
                             col(4), _rot_cols(col(4), DIFF_HD), col(5), _rot_cols(col(5), DIFF_HD),
                             col(6), col(7)], axis=-1).astype(BF16)
    e_sizes = (ssm_w, q_lora, kv_lora, LANE, LANE, dq_w, dq_w, dk_w, dk_w, dv_w, n_br * dm)
    e_offs = [0]
    for s in e_sizes:
        e_offs.append(e_offs[-1] + s)

    step = jnp.exp(ssm_log_dt)[..., None]
    mag = jnp.exp(ssm_lambda_re * step)
    a_re = mag * jnp.cos(ssm_lambda_im * step)
    a_im = mag * jnp.sin(ssm_lambda_im * step)
    den = ssm_lambda_re * ssm_lambda_re + ssm_lambda_im * ssm_lambda_im
    z_re = ((a_re - 1.0) * ssm_lambda_re + a_im * ssm_lambda_im) / den
    z_im = (a_im * ssm_lambda_re - (a_re - 1.0) * ssm_lambda_im) / den
    bb_re = z_re[..., None] * ssm_b_re - z_im[..., None] * ssm_b_im
    bb_im = z_re[..., None] * ssm_b_im + z_im[..., None] * ssm_b_re
    bd = jax.vmap(_block_diag)
    wb = jnp.concatenate([bd(jnp.swapaxes(bb_re, -1, -2)), bd(jnp.swapaxes(bb_im, -1, -2))],
                         axis=-1).astype(BF16)
    wc = jnp.concatenate([bd(jnp.swapaxes(ssm_c_re, -1, -2)), -bd(jnp.swapaxes(ssm_c_im, -1, -2))],
                         axis=-2).astype(BF16)
    a_vec = jnp.stack([a_re.reshape(depth, ns), a_im.reshape(depth, ns)], axis=1).reshape(depth, 1, 2 * ns)
    p_re, p_im = _cplx_pow2(a_re, a_im, int(math.log2(chunk_len)))
    a_pow = jnp.stack([p_re.reshape(depth, ns), p_im.reshape(depth, ns)], axis=1).reshape(depth, 1, 2 * ns)
    d_vec = ssm_d.reshape(depth, 1, ssm_w)
    w_glu = ssm_w_glu.astype(BF16)

    w_uq_nope = mla_w_uq.reshape(depth, q_lora, MLA_HEADS, mla_qd)[..., :MLA_NOPE].reshape(depth, q_lora, -1)
    w_uq_rope = mla_w_uq.reshape(depth, q_lora, MLA_HEADS, mla_qd)[..., MLA_NOPE:].reshape(depth, q_lora, -1)
    head_pad = lambda w: lane_pad(w.reshape(depth, q_lora, MLA_HEADS, MLA_ROPE)).reshape(depth, q_lora, -1)
    w_uq_ext = jnp.concatenate([w_uq_nope, head_pad(w_uq_rope), head_pad(_rot_cols(w_uq_rope, MLA_ROPE))],
                               axis=-1).astype(BF16)
    w_uk_bd = bd(jnp.transpose(mla_w_uk, (0, 2, 3, 1))).astype(BF16)
    w_uv_bd = bd(jnp.transpose(mla_w_uv, (0, 2, 1, 3))).astype(BF16)

    lam = (jnp.exp(jnp.sum(diff_lambda_q1 * diff_lambda_k1, axis=-1))
           - jnp.exp(jnp.sum(diff_lambda_q2 * diff_lambda_k2, axis=-1)))
    w_bs, w_bm, w_bd_, w_o = (w.astype(BF16) for w in (w_br_ssm, w_br_mla, w_br_diff, w_out))
    w_u, w_dn = w_up.astype(BF16), w_down.astype(BF16)
    bg = b_gate.reshape(depth, 1, n_br * dm)

    pos = jnp.concatenate([jnp.tile(jnp.arange(seq), nbp), jnp.tile(past_len + jnp.arange(steps), nbs)])
    cos32, sin32 = _rope_tables(pos, MLA_ROPE)
    cos64, sin64 = _rope_tables(pos, DIFF_HD)
    c64, s64 = jnp.tile(cos64, (1, LANE // DIFF_HD)), jnp.tile(sin64, (1, LANE // DIFF_HD))
    c32, s32 = lane_pad(cos32), lane_pad(sin32)

    cache_krt = jnp.transpose(cache_mla_krope, (0, 1, 3, 2))
    cache_dkt = jnp.transpose(cache_diff_k, (0, 1, 3, 4, 5, 2)).reshape(depth, n_pool, dk_w, page)
    cache_dv2 = cache_diff_v.reshape(depth, n_pool, page * DIFF_KV_HEADS, DIFF_VD)
    s0_grp = jnp.concatenate([state_ssm_re.reshape(depth, nbs, ns), state_ssm_im.reshape(depth, nbs, ns)],
                             axis=-1).reshape(depth, nbs // SUBLANE, SUBLANE, 2 * ns)

    x = jnp.concatenate([x_prompt.reshape(tp, dm), x_sample.reshape(ts, dm)], axis=0)
    rows = [[] for _ in range(12)]
    for l in range(depth):
        u, cq_n, ckv_n, kr128, kpad, qd_blk, kd, kd_bf, dv, dv_bf, gz = _in_proj(
            x, g_mix[l][None], w_ext[l], mla_g_q[l][None], mla_g_kv[l][None], c64, s64, c32, s32,
            e_offs, diff_scale * LOG2E, tm_in)
        kr = kr128[:, :MLA_ROPE]

        u_p = u[:tp].reshape(nbp, seq // tl, SUBLANE, chunk_len, ssm_w)
        u_p = jnp.swapaxes(u_p, 2, 3).reshape(nbp, seq, ssm_w)
        y_p, fin_p = _ssm_prompt(u_p, wb[l], a_vec[l], a_pow[l], wc[l], d_vec[l], tl)
        y_p = jnp.swapaxes(y_p.reshape(nbp, seq // tl, chunk_len, SUBLANE, ssm_w), 2, 3).reshape(tp, ssm_w)
        u_s = u[tp:].reshape(nbs // SUBLANE, SUBLANE, steps, ssm_w)
        u_s = jnp.swapaxes(u_s, 1, 2).reshape(nbs // SUBLANE, steps * SUBLANE, ssm_w)
        y_s, fin_s = _ssm_sample(u_s, s0_grp[l], wb[l], a_vec[l], wc[l], d_vec[l], steps)
        y_s = jnp.swapaxes(y_s.reshape(nbs // SUBLANE, steps, SUBLANE, ssm_w), 1, 2).reshape(ts, ssm_w)
        y = jax.nn.gelu(jnp.concatenate([y_p, y_s], axis=0))
        y_ssm = y * jax.nn.sigmoid(_mm(y, w_glu[l], tm) + ssm_b_glu[l])
        fin_p = fin_p.reshape(nbp, 2, n_grp, SSM_STATE)
        fin_s = fin_s.reshape(nbs, 2, n_grp, SSM_STATE)

        q_pad = _q_proj(cq_n, w_uq_ext[l], w_uk_bd[l], c32, s32, MLA_HEADS * MLA_NOPE, MLA_HEADS,
                        mla_scale * LOG2E, tm)
        no_par = jnp.zeros((SUBLANE, LANE), F32)
        o_p = _flash(q_pad, kpad, kpad, no_par, nbp, seq, MLA_HEADS, 1, tq, tk, False)

        lam_init = 0.8 - 0.6 * math.exp(-0.3 * l)
        lam_full = lam[l] + lam_init
        par = jnp.concatenate([jnp.full((1, LANE), lam_full, F32), diff_g_sub[l][None],
                               jnp.full((1, LANE), 1.0 - lam_init, F32),
                               jnp.zeros((SUBLANE - 3, LANE), F32)], axis=0)
        y_diff_p = _flash(qd_blk, kd_bf, dv_bf, par, nbp, seq, DIFF_GROUP * 2, DIFF_KV_HEADS, tq, tk, True)

        by_tok = lambda a: a[tp:].reshape(nbs, steps, -1)
        pad_rows = lambda a: jnp.pad(a, ((0, 0), (0, page * (a.shape[1] // steps) - a.shape[1]), (0, 0)))
        pad_cols = lambda a: jnp.pad(jnp.swapaxes(a, 1, 2), ((0, 0), (0, 0), (0, page - steps)))
        q_s = jnp.transpose(q_pad[tp:].reshape(nbs, steps, MLA_HEADS, kv_lora + LANE), (0, 2, 1, 3))
        q_s = q_s.reshape(nbs, MLA_HEADS * steps, kv_lora + LANE)
        ql_s = q_s[:, :, :kv_lora]
        qr_s = q_s[:, :, kv_lora:kv_lora + MLA_ROPE]
        qd_s = jnp.transpose(qd_blk[tp:].reshape(nbs, steps, DIFF_KV_HEADS, DIFF_GROUP * 2, 2 * DIFF_HD),
                             (0, 2, 3, 1, 4)).reshape(nbs, DIFF_KV_HEADS, DIFF_GROUP * 2 * steps, 2 * DIFF_HD)
        new_dv = by_tok(dv).reshape(nbs, steps * DIFF_KV_HEADS, DIFF_VD)
        o_all = _paged(page_table, l, ql_s, qr_s, qd_s, pad_rows(by_tok(ckv_n)), pad_cols(by_tok(kr)),
                       pad_cols(by_tok(kd)), pad_rows(new_dv),
                       cache_mla_latent, cache_krt, cache_dkt, cache_dv2, steps)
        rows_m = MLA_HEADS * steps
        o_s = jnp.transpose(o_all[:, :rows_m].reshape(nbs, MLA_HEADS, steps, kv_lora), (0, 2, 1, 3))
        o_s = o_s.reshape(ts, MLA_HEADS * kv_lora)
        od_s = jnp.transpose(o_all[:, rows_m:].reshape(nbs, DIFF_KV_HEADS, DIFF_GROUP, 2, steps, DIFF_VD),
                             (0, 4, 1, 2, 3, 5))
        od_s = od_s.reshape(ts, DIFF_HEADS, 2, DIFF_VD)
        od_s = od_s[:, :, 0, :] - lam_full * od_s[:, :, 1, :]
        y_diff_s = (_rms(od_s, diff_g_sub[l]) * (1.0 - lam_init)).reshape(ts, DIFF_HEADS * DIFF_VD)

        o_mla = jnp.concatenate([o_p, o_s], axis=0)
        y_diff = jnp.concatenate([y_diff_p, y_diff_s], axis=0)
        x = _merge(x, y_ssm, o_mla, y_diff, gz, bg[l], w_uv_bd[l], w_bs[l], w_bm[l], w_bd_[l], w_o[l], tm)
        x = _mlp(x, g_mlp[l][None], w_u[l], w_dn[l], tm, 1024)

        new = (ckv_n, kr, kd, dv)
        shapes = ((kv_lora,), (MLA_ROPE,), (DIFF_KV_HEADS, 2, DIFF_HD), (DIFF_KV_HEADS, DIFF_VD))
        for i, (a, shp) in enumerate(zip(new, shapes)):
            rows[i].append(a[:tp].reshape((nbp, seq) + shp))
            rows[6 + i].append(a[tp:].reshape((nbs, steps) + shp))
        rows[4].append(fin_p[:, 0])
        rows[5].append(fin_p[:, 1])
        rows[10].append(fin_s[:, 0])
        rows[11].append(fin_s[:, 1])

    y = _norm(x, g_final[None], tm)
    outs = [jnp.stack(r) for r in rows]
    return (y[:tp].reshape(nbp, seq, dm), y[tp:].reshape(nbs, steps, dm), *outs)
```

```python
import functools
import math

import jax
import jax.numpy as jnp
from jax import lax
from jax.experimental import pallas as pl
from jax.experimental.pallas import tpu as pltpu

F32 = jnp.float32
BF16 = jnp.bfloat16

SSM_GROUP = 16
SSM_STATE = 64
MLA_HEADS = 8
MLA_NOPE = 64
MLA_ROPE = 32
DIFF_HEADS = 4
DIFF_KV_HEADS = 2
DIFF_GROUP = DIFF_HEADS // DIFF_KV_HEADS
DIFF_HD = 64
DIFF_VD = 2 * DIFF_HD
ROPE_THETA = 10000.0
EPS = 1e-6

LANE = 128
SUBLANE = 8
VMEM_LIMIT = 56 * 1024 * 1024


def _cparams(*sem):
    return pltpu.CompilerParams(dimension_semantics=sem, vmem_limit_bytes=VMEM_LIMIT)


def _rms(x, g):
    ms = jnp.mean(x * x, axis=-1, keepdims=True)
    return x * lax.rsqrt(ms + EPS) * g


def _in_proj_kernel(x_ref, g_ref, w_ref, gq_ref, gkv_ref, c64_ref, s64_ref, c32_ref, s32_ref,
                    u_ref, cq_ref, ckv_ref, kr_ref, kpad_ref, qd_ref, kd_ref, kdb_ref, dv_ref, dvb_ref, gz_ref,
                    *, offs, q_scale):
    h = _rms(x_ref[...], g_ref[...]).astype(BF16)

    def proj(i):
        return jnp.dot(h, w_ref[:, offs[i]:offs[i + 1]], preferred_element_type=F32)

    u_ref[...] = proj(0)
    cq_ref[...] = _rms(proj(1), gq_ref[...]).astype(BF16)
    ckv = _rms(proj(2), gkv_ref[...])
    ckv_ref[...] = ckv
    kr = proj(3) * c32_ref[...] + proj(4) * s32_ref[...]
    kr_ref[...] = kr
    kpad_ref[...] = jnp.concatenate([ckv, kr], axis=1).astype(BF16)

    c64 = c64_ref[...]
    s64 = s64_ref[...]
    dq = proj(5)
    n_q = dq.shape[1] // LANE
    qd = (dq * jnp.tile(c64, (1, n_q)) + proj(6) * jnp.tile(s64, (1, n_q))) * q_scale
    low = lax.broadcasted_iota(jnp.int32, (dq.shape[0], LANE), 1) < DIFF_HD
    blocks = []
    for j in range(n_q):
        v = qd[:, j * LANE:(j + 1) * LANE]
        blocks += [jnp.where(low, v, 0.0), jnp.where(low, 0.0, v)]
    qd_ref[...] = jnp.concatenate(blocks, axis=1).astype(BF16)
    dk = proj(7)
    n_k = dk.shape[1] // LANE
    kd = dk * jnp.tile(c64, (1, n_k)) + proj(8) * jnp.tile(s64, (1, n_k))
    kd_ref[...] = kd
    kdb_ref[...] = kd.astype(BF16)
    dv = proj(9)
    dv_ref[...] = dv
    dvb_ref[...] = dv.astype(BF16)
    gz_ref[...] = proj(10)


def _in_proj(x, g, w, g_q, g_kv, c64, s64, c32, s32, offs, q_scale, tm):
    t, k = x.shape
    widths = [offs[i + 1] - offs[i] for i in range(len(offs) - 1)]
    row = lambda i: (i, 0)
    const = lambda i: (0, 0)
    outs = [(widths[0], F32), (widths[1], BF16), (widths[2], F32), (LANE, F32), (widths[2] + LANE, BF16),
            (2 * widths[5], BF16), (widths[7], F32), (widths[7], BF16), (widths[9], F32), (widths[9], BF16),
            (widths[10], F32)]
    return pl.pallas_call(
        functools.partial(_in_proj_kernel, offs=tuple(offs), q_scale=q_scale),
        grid=(t // tm,),
        in_specs=[pl.BlockSpec((tm, k), row), pl.BlockSpec((1, k), const), pl.BlockSpec(w.shape, const),
                  pl.BlockSpec(g_q.shape, const), pl.BlockSpec(g_kv.shape, const)]
                 + [pl.BlockSpec((tm, LANE), row)] * 4,
        out_specs=[pl.BlockSpec((tm, n), row) for n, _ in outs],
        out_shape=[jax.ShapeDtypeStruct((t, n), dt) for n, dt in outs],
        compiler_params=_cparams("parallel"),
        name="in_proj",
    )(x, g, w, g_q, g_kv, c64, s64, c32, s32)


def _q_proj_kernel(cq_ref, wq_ref, wuk_ref, c32_ref, s32_ref, o_ref, *, n_nope, n_heads, q_scale):
    q = jnp.dot(cq_ref[...], wq_ref[...], preferred_element_type=F32)
    q_lat = jnp.dot(q[:, :n_nope].astype(BF16), wuk_ref[...], preferred_element_type=F32) * q_scale
    n_r = n_heads * LANE
    rope = (q[:, n_nope:n_nope + n_r] * jnp.tile(c32_ref[...], (1, n_heads))
            + q[:, n_nope + n_r:] * jnp.tile(s32_ref[...], (1, n_heads))) * q_scale
    blocks = []
    for hd in range(n_heads):
        blocks += [q_lat[:, hd * LANE:(hd + 1) * LANE], rope[:, hd * LANE:(hd + 1) * LANE]]
    o_ref[...] = jnp.concatenate(blocks, axis=1).astype(BF16)


def _q_proj(cq, wq, wuk, c32, s32, n_nope, n_heads, q_scale, tm):
    t, k = cq.shape
    row = lambda i: (i, 0)
    const = lambda i: (0, 0)
    n_out = n_heads * 2 * LANE
    return pl.pallas_call(
        functools.partial(_q_proj_kernel, n_nope=n_nope, n_heads=n_heads, q_scale=q_scale),
        grid=(t // tm,),
        in_specs=[pl.BlockSpec((tm, k), row), pl.BlockSpec(wq.shape, const), pl.BlockSpec(wuk.shape, const),
                  pl.BlockSpec((tm, LANE), row), pl.BlockSpec((tm, LANE), row)],
        out_specs=pl.BlockSpec((tm, n_out), row),
        out_shape=jax.ShapeDtypeStruct((t, n_out), BF16),
        compiler_params=_cparams("parallel"),
        name="q_proj",
    )(cq, wq, wuk, c32, s32)


def _merge_kernel(x_ref, ys_ref, om_ref, yd_ref, gz_ref, bg_ref, wuv_ref, ws_ref, wm_ref, wd_ref, wo_ref, o_ref):
    d = x_ref.shape[1]
    y_mla = jnp.dot(om_ref[...].astype(BF16), wuv_ref[...], preferred_element_type=F32)
    merged = None
    for b, (y, w_ref) in enumerate(((ys_ref[...], ws_ref), (y_mla, wm_ref), (yd_ref[...], wd_ref))):
        gate = jax.nn.sigmoid(gz_ref[:, b * d:(b + 1) * d] + bg_ref[:, b * d:(b + 1) * d])
        term = gate * jnp.dot(y.astype(BF16), w_ref[...], preferred_element_type=F32)
        merged = term if merged is None else merged + term
    o_ref[...] = x_ref[...] + jnp.dot(merged.astype(BF16), wo_ref[...], preferred_element_type=F32)


def _merge(x, y_ssm, y_mla, y_diff, gz, b_gate, w_uv, w_s, w_m, w_d, w_o, tm):
    t, d = x.shape
    row = lambda i: (i, 0)
    const = lambda i: (0, 0)
    return pl.pallas_call(
        _merge_kernel,
        grid=(t // tm,),
        in_specs=[pl.BlockSpec((tm, d), row),
                  pl.BlockSpec((tm, y_ssm.shape[1]), row),
                  pl.BlockSpec((tm, y_mla.shape[1]), row),
                  pl.BlockSpec((tm, y_diff.shape[1]), row),
                  pl.BlockSpec((tm, gz.shape[1]), row),
                  pl.BlockSpec(b_gate.shape, const),
                  pl.BlockSpec(w_uv.shape, const),
                  pl.BlockSpec(w_s.shape, const),
                  pl.BlockSpec(w_m.shape, const),
                  pl.BlockSpec(w_d.shape, const),
                  pl.BlockSpec(w_o.shape, const)],
        out_specs=pl.BlockSpec((tm, d), row),
        out_shape=jax.ShapeDtypeStruct((t, d), F32),
        compiler_params=_cparams("parallel"),
        name="merge",
    )(x, y_ssm, y_mla, y_diff, gz, b_gate, w_uv, w_s, w_m, w_d, w_o)


def _mlp_kernel(x_ref, g_ref, wu_ref, wd_ref, o_ref, h_sc):
    f = pl.program_id(1)

    @pl.when(f == 0)
    def _():
        x = x_ref[...]
        h_sc[...] = _rms(x, g_ref[...]).astype(BF16)
        o_ref[...] = x

    a = jnp.dot(h_sc[...], wu_ref[...], preferred_element_type=F32)
    a = jnp.square(jnp.maximum(a, 0.0)).astype(BF16)
    o_ref[...] += jnp.dot(a, wd_ref[...], preferred_element_type=F32)


def _mlp(x, g, w_up, w_down, tm, tf):
    t, d = x.shape
    ff = w_up.shape[1]
    return pl.pallas_call(
        _mlp_kernel,
        grid=(t // tm, ff // tf),
        in_specs=[pl.BlockSpec((tm, d), lambda i, f: (i, 0)),
                  pl.BlockSpec((1, d), lambda i, f: (0, 0)),
                  pl.BlockSpec((d, tf), lambda i, f: (0, f)),
                  pl.BlockSpec((tf, d), lambda i, f: (f, 0))],
        out_specs=pl.BlockSpec((tm, d), lambda i, f: (i, 0)),
        out_shape=jax.ShapeDtypeStruct((t, d), F32),
        scratch_shapes=[pltpu.VMEM((tm, d), BF16)],
        compiler_params=_cparams("parallel", "arbitrary"),
        name="mlp",
    )(x, g, w_up, w_down)


def _norm_kernel(x_ref, g_ref, o_ref):
    o_ref[...] = _rms(x_ref[...], g_ref[...])


def _norm(x, g, tm):
    t, d = x.shape
    return pl.pallas_call(
        _norm_kernel,
        grid=(t // tm,),
        in_specs=[pl.BlockSpec((tm, d), lambda i: (i, 0)), pl.BlockSpec((1, d), lambda i: (0, 0))],
        out_specs=pl.BlockSpec((tm, d), lambda i: (i, 0)),
        out_shape=jax.ShapeDtypeStruct((t, d), F32),
        compiler_params=_cparams("parallel"),
        name="final_norm",
    )(x, g)


SCAN_COLS = 4 * LANE


def _cplx_step(ar, ai, sr, si, xr, xi):
    return ar * sr - ai * si + xr, ar * si + ai * sr + xi


def _ssm_out(xs, u, wc_ref, d_ref, wg_ref, bg_ref):
    y = jnp.dot(xs[...].astype(BF16), wc_ref[...], preferred_element_type=F32) + d_ref[...] * u
    y = jax.nn.gelu(y)
    return y * jax.nn.sigmoid(jnp.dot(y.astype(BF16), wg_ref[...], preferred_element_type=F32) + bg_ref[...])


def _ssm_prompt_kernel(u_ref, wb_ref, a_ref, ap_ref, wc_ref, d_ref, wg_ref, bg_ref, y_ref, fin_ref,
                       xs, state_sc, carry_sc, fin_sc, *, chunk_len, ns):
    li = pl.program_id(1)

    @pl.when(li == 0)
    def _():
        state_sc[...] = jnp.zeros_like(state_sc)

    u = u_ref[0]
    xs[...] = jnp.dot(u.astype(BF16), wb_ref[...], preferred_element_type=F32)

    n_cb = ns // SCAN_COLS
    for cb in range(n_cb):
        re = pl.ds(cb * SCAN_COLS, SCAN_COLS)
        im = pl.ds(ns + cb * SCAN_COLS, SCAN_COLS)
        ar = jnp.broadcast_to(a_ref[0:1, re], (SUBLANE, SCAN_COLS))
        ai = jnp.broadcast_to(a_ref[0:1, im], (SUBLANE, SCAN_COLS))

        def sweep1(t, s):
            rows = pl.ds(pl.multiple_of(t * SUBLANE, SUBLANE), SUBLANE)
            return _cplx_step(ar, ai, s[0], s[1], xs[rows, re], xs[rows, im])

        zero = jnp.zeros((SUBLANE, SCAN_COLS), F32)
        fr, fi = lax.fori_loop(0, chunk_len, sweep1, (zero, zero))
        fin_sc[:, re] = fr
        fin_sc[:, im] = fi

    carry_sc[0:1, :] = state_sc[0:1, :]
    pr = ap_ref[0:1, 0:ns]
    pi = ap_ref[0:1, ns:2 * ns]
    for c in range(1, SUBLANE):
        cr = carry_sc[c - 1:c, 0:ns]
        ci = carry_sc[c - 1:c, ns:2 * ns]
        nr, ni = _cplx_step(pr, pi, cr, ci, fin_sc[c - 1:c, 0:ns], fin_sc[c - 1:c, ns:2 * ns])
        carry_sc[c:c + 1, 0:ns] = nr
        carry_sc[c:c + 1, ns:2 * ns] = ni

    for cb in range(n_cb):
        re = pl.ds(cb * SCAN_COLS, SCAN_COLS)
        im = pl.ds(ns + cb * SCAN_COLS, SCAN_COLS)
        ar = jnp.broadcast_to(a_ref[0:1, re], (SUBLANE, SCAN_COLS))
        ai = jnp.broadcast_to(a_ref[0:1, im], (SUBLANE, SCAN_COLS))

        def sweep2(t, s):
            rows = pl.ds(pl.multiple_of(t * SUBLANE, SUBLANE), SUBLANE)
            nr, ni = _cplx_step(ar, ai, s[0], s[1], xs[rows, re], xs[rows, im])
            xs[rows, re] = nr
            xs[rows, im] = ni
            return nr, ni

        fr, fi = lax.fori_loop(0, chunk_len, sweep2, (carry_sc[:, re], carry_sc[:, im]))
        state_sc[0:1, re] = fr[SUBLANE - 1:SUBLANE]
        state_sc[0:1, im] = fi[SUBLANE - 1:SUBLANE]

    y_ref[0] = _ssm_out(xs, u, wc_ref, d_ref, wg_ref, bg_ref)
    fin_ref[0] = state_sc[0:1, :]


def _ssm_prompt(u_perm, wb, a_vec, a_pow, wc, d_vec, w_glu, b_glu, tl):
    nb, seq, w = u_perm.shape
    ns2 = wb.shape[1]
    const = lambda b, i: (0, 0)
    kern = functools.partial(_ssm_prompt_kernel, chunk_len=tl // SUBLANE, ns=ns2 // 2)
    return pl.pallas_call(
        kern,
        grid=(nb, seq // tl),
        in_specs=[pl.BlockSpec((1, tl, w), lambda b, i: (b, i, 0)),
                  pl.BlockSpec(wb.shape, const),
                  pl.BlockSpec(a_vec.shape, const),
                  pl.BlockSpec(a_pow.shape, const),
                  pl.BlockSpec(wc.shape, const),
                  pl.BlockSpec(d_vec.shape, const),
                  pl.BlockSpec(w_glu.shape, const),
                  pl.BlockSpec(b_glu.shape, const)],
        out_specs=[pl.BlockSpec((1, tl, w), lambda b, i: (b, i, 0)),
                   pl.BlockSpec((1, 1, ns2), lambda b, i: (b, 0, 0))],
        out_shape=[jax.ShapeDtypeStruct((nb, seq, w), F32),
                   jax.ShapeDtypeStruct((nb, 1, ns2), F32)],
        scratch_shapes=[pltpu.VMEM((tl, ns2), F32),
                        pltpu.VMEM((SUBLANE, ns2), F32),
                        pltpu.VMEM((SUBLANE, ns2), F32),
                        pltpu.VMEM((SUBLANE, ns2), F32)],
        compiler_params=_cparams("parallel", "arbitrary"),
        name="ssm_prompt",
    )(u_perm, wb, a_vec, a_pow, wc, d_vec, w_glu, b_glu)


def _ssm_sample_kernel(u_ref, s0_ref, wb_ref, a_ref, wc_ref, d_ref, wg_ref, bg_ref, y_ref, fin_ref, xs,
                       *, steps, ns):
    u = u_ref[0]
    xs[...] = jnp.dot(u.astype(BF16), wb_ref[...], preferred_element_type=F32)
    for cb in range(ns // SCAN_COLS):
        re = pl.ds(cb * SCAN_COLS, SCAN_COLS)
        im = pl.ds(ns + cb * SCAN_COLS, SCAN_COLS)
        ar = jnp.broadcast_to(a_ref[0:1, re], (SUBLANE, SCAN_COLS))
        ai = jnp.broadcast_to(a_ref[0:1, im], (SUBLANE, SCAN_COLS))
        sr = s0_ref[0, :, re]
        si = s0_ref[0, :, im]
        for t in range(steps):
            rows = pl.ds(t * SUBLANE, SUBLANE)
            sr, si = _cplx_step(ar, ai, sr, si, xs[rows, re], xs[rows, im])
            xs[rows, re] = sr
            xs[rows, im] = si
        fin_ref[0, :, re] = sr
        fin_ref[0, :, im] = si
    y_ref[0] = _ssm_out(xs, u, wc_ref, d_ref, wg_ref, bg_ref)


def _ssm_sample(u_grp, s0_grp, wb, a_vec, wc, d_vec, w_glu, b_glu, steps):
    ng, rows, w = u_grp.shape
    ns2 = wb.shape[1]
    const = lambda g: (0, 0)
    kern = functools.partial(_ssm_sample_kernel, steps=steps, ns=ns2 // 2)
    return pl.pallas_call(
        kern,
        grid=(ng,),
        in_specs=[pl.BlockSpec((1, rows, w), lambda g: (g, 0, 0)),
                  pl.BlockSpec((1, SUBLANE, ns2), lambda g: (g, 0, 0)),
                  pl.BlockSpec(wb.shape, const),
                  pl.BlockSpec(a_vec.shape, const),
                  pl.BlockSpec(wc.shape, const),
                  pl.BlockSpec(d_vec.shape, const),
                  pl.BlockSpec(w_glu.shape, const),
                  pl.BlockSpec(b_glu.shape, const)],
        out_specs=[pl.BlockSpec((1, rows, w), lambda g: (g, 0, 0)),
                   pl.BlockSpec((1, SUBLANE, ns2), lambda g: (g, 0, 0))],
        out_shape=[jax.ShapeDtypeStruct((ng, rows, w), F32),
                   jax.ShapeDtypeStruct((ng, SUBLANE, ns2), F32)],
        scratch_shapes=[pltpu.VMEM((rows, ns2), F32)],
        compiler_params=_cparams("parallel"),
        name="ssm_sample",
    )(u_grp, s0_grp, wb, a_vec, wc, d_vec, w_glu, b_glu)


def _nt_dot(a, b):
    return lax.dot_general(a, b, (((1,), (1,)), ((), ())), preferred_element_type=F32)


def _softmax_step(s, v, m_sc, l_sc, acc_sc):
    reps = s.shape[1] // LANE
    m_prev = m_sc[...]
    m_next = jnp.maximum(m_prev, jnp.max(s, axis=1, keepdims=True))
    p = jnp.exp2(s - jnp.tile(m_next, (1, reps)))
    alpha = jnp.exp2(m_prev - m_next)
    l_sc[...] = alpha * l_sc[...] + jnp.sum(p, axis=1, keepdims=True)
    acc_sc[...] = alpha * acc_sc[...] + jnp.dot(p.astype(BF16), v, preferred_element_type=F32)
    m_sc[...] = m_next


def _flash_kernel(q_ref, k_ref, v_ref, par_ref, o_ref, m_sc, l_sc, acc_sc, *, n_stack, tk, diff):
    tq = q_ref.shape[0]
    d = k_ref.shape[1]
    i = pl.program_id(1)
    q = jnp.concatenate([q_ref[:, s * d:(s + 1) * d] for s in range(n_stack)], axis=0)
    m_sc[...] = jnp.full_like(m_sc, -jnp.inf)
    l_sc[...] = jnp.zeros_like(l_sc)
    acc_sc[...] = jnp.zeros_like(acc_sc)
    n_full = (i * tq) // tk
    first_key = n_full * tk

    def chunk(j):
        keys = pl.ds(pl.multiple_of(j * tk, tk), tk)
        return _nt_dot(q, k_ref[keys, :]), v_ref[keys, :]

    def body(j, carry):
        s, v = chunk(j)
        _softmax_step(s, v, m_sc, l_sc, acc_sc)
        return carry

    lax.fori_loop(0, n_full, body, 0)

    s, v = chunk(n_full)
    row = lax.broadcasted_iota(jnp.int32, (tq, tk), 0) + (i * tq - first_key)
    col = lax.broadcasted_iota(jnp.int32, (tq, tk), 1)
    s = jnp.where((col <= row)[None], s.reshape(n_stack, tq, tk), -jnp.inf).reshape(n_stack * tq, tk)
    _softmax_step(s, v, m_sc, l_sc, acc_sc)
    o = acc_sc[...] / l_sc[...]
    parts = [o[s * tq:(s + 1) * tq] for s in range(n_stack)]
    if diff:
        lam, gain, post = par_ref[0:1, :], par_ref[1:2, :], par_ref[2:3, :]
        parts = [_rms(parts[2 * g] - lam * parts[2 * g + 1], gain) * post for g in range(n_stack // 2)]
    o_ref[...] = jnp.concatenate(parts, axis=1)


def _flash(q, k, v, par, n_seq, seq, n_stack, n_col, tq, tk, diff):
    d = q.shape[1] // (n_col * n_stack)
    n_q = seq // tq
    out_w = n_stack * LANE // (2 if diff else 1)
    return pl.pallas_call(
        functools.partial(_flash_kernel, n_stack=n_stack, tk=tk, diff=diff),
        grid=(n_seq * n_col, n_q),
        in_specs=[pl.BlockSpec((tq, n_stack * d), lambda b, i: ((b // n_col) * n_q + i, b % n_col)),
                  pl.BlockSpec((seq, d), lambda b, i: (b // n_col, b % n_col)),
                  pl.BlockSpec((seq, LANE), lambda b, i: (b // n_col, b % n_col)),
                  pl.BlockSpec(par.shape, lambda b, i: (0, 0))],
        out_specs=pl.BlockSpec((tq, out_w), lambda b, i: ((b // n_col) * n_q + i, b % n_col)),
        out_shape=jax.ShapeDtypeStruct((n_seq * seq, n_col * out_w), F32),
        scratch_shapes=[pltpu.VMEM((n_stack * tq, LANE), F32)] * 3,
        compiler_params=_cparams("parallel", "parallel"),
        name="flash",
    )(q, k, v, par)


PAGES_PER_STEP = 16
LOG2E = math.log2(math.e)


def _paged_kernel(pt_ref, ql_ref, qr_ref, qd_ref, nlat_ref, nkr_ref, ndk_ref, ndv_ref, *rest,
                  n_pages, steps):
    pages = rest[:4 * n_pages]
    o_ref, m_sc, l_sc, acc_sc = rest[4 * n_pages:]
    g = pl.program_id(1)
    n_kv = qd_ref.shape[1]
    rows_m = ql_ref.shape[1]
    rows_d = qd_ref.shape[2]
    page = pages[0].shape[0]
    hd2 = qd_ref.shape[3]

    @pl.when(g == 0)
    def _():
        m_sc[...] = jnp.full_like(m_sc, -jnp.inf)
        l_sc[...] = jnp.zeros_like(l_sc)
        acc_sc[...] = jnp.zeros_like(acc_sc)

    def cached(lat_refs, kr_refs, dk_refs, dv_refs):
        lat = jnp.concatenate([r[...].astype(BF16) for r in lat_refs], 0)
        krt = jnp.concatenate([r[...].astype(BF16) for r in kr_refs], 1)
        scores = [_nt_dot(ql_ref[0], lat) + jnp.dot(qr_ref[0], krt, preferred_element_type=F32)]
        values = [lat]
        for kv in range(n_kv):
            dkt = jnp.concatenate([r[kv * hd2:(kv + 1) * hd2, :].astype(BF16) for r in dk_refs], 1)
            scores.append(jnp.dot(qd_ref[0, kv], dkt, preferred_element_type=F32))
            values.append(jnp.concatenate([r[pl.ds(kv, page, stride=n_kv), :].astype(BF16) for r in dv_refs], 0))
        return scores, values

    def fresh():
        def page_of(ref):
            a = ref[0]
            return jnp.concatenate([a, jnp.zeros((page - a.shape[0], a.shape[1]), a.dtype)], axis=0).astype(BF16)

        lat, kr, dk, dv = page_of(nlat_ref), page_of(nkr_ref), page_of(ndk_ref), page_of(ndv_ref)
        scores = [_nt_dot(ql_ref[0], lat) + _nt_dot(qr_ref[0], kr)]
        values = [lat]
        for kv in range(n_kv):
            scores.append(_nt_dot(qd_ref[0, kv], dk[:, kv * hd2:(kv + 1) * hd2]))
            values.append(dv[:, kv * LANE:(kv + 1) * LANE])
        return scores, values

    def attend(scores, values, mask):
        s = jnp.concatenate(scores, axis=0)
        if mask:
            row = lax.broadcasted_iota(jnp.int32, s.shape, 0) % steps
            col = lax.broadcasted_iota(jnp.int32, s.shape, 1)
            s = jnp.where(col <= row, s, -jnp.inf)
        reps = s.shape[1] // LANE
        m_prev = m_sc[...]
        m_next = jnp.maximum(m_prev, jnp.max(s, axis=1, keepdims=True))
        p = jnp.exp2(s - jnp.tile(m_next, (1, reps)))
        alpha = jnp.exp2(m_prev - m_next)
        l_sc[...] = alpha * l_sc[...] + jnp.sum(p, axis=1, keepdims=True)
        p = p.astype(BF16)
        bounds = [0, rows_m] + [rows_m + (kv + 1) * rows_d for kv in range(n_kv)]
        pv = [jnp.dot(p[lo:hi], v, preferred_element_type=F32)
              for lo, hi, v in zip(bounds[:-1], bounds[1:], values)]
        acc_sc[...] = alpha * acc_sc[...] + jnp.concatenate(pv, axis=0)
        m_sc[...] = m_next

    attend(*cached(pages[0:n_pages], pages[n_pages:2 * n_pages], pages[2 * n_pages:3 * n_pages],
                   pages[3 * n_pages:4 * n_pages]), False)

    @pl.when(g == pl.num_programs(1) - 1)
    def _():
        attend(*fresh(), True)
        o_ref[0] = acc_sc[...] / l_sc[...]


def _paged(page_table, layer, q_lat, q_rope, q_diff, new_lat, new_krt, new_dkt, new_dv,
           cache_lat, cache_krt, cache_dkt, cache_dv, steps):
    nb, n_pages_total = page_table.shape
    npg = math.gcd(PAGES_PER_STEP, n_pages_total)
    n_kv = q_diff.shape[1]
    rows = q_lat.shape[1] + n_kv * q_diff.shape[2]

    def page_spec(cache, p):
        return pl.BlockSpec((None, None) + cache.shape[2:],
                            lambda b, g, pt: (layer, pt[b, g * npg + p], 0, 0))

    per_b3 = lambda b, g, pt: (b, 0, 0)
    per_b4 = lambda b, g, pt: (b, 0, 0, 0)
    operands = [q_lat, q_rope, q_diff, new_lat, new_krt, new_dkt, new_dv]
    in_specs = [pl.BlockSpec((1,) + a.shape[1:], per_b4 if a.ndim == 4 else per_b3) for a in operands]
    for cache in (cache_lat, cache_krt, cache_dkt, cache_dv):
        for p in range(npg):
            in_specs.append(page_spec(cache, p))
            operands.append(cache)
    kern = functools.partial(_paged_kernel, n_pages=npg, steps=steps)
    grid_spec = pltpu.PrefetchScalarGridSpec(
        num_scalar_prefetch=1,
        grid=(nb, n_pages_total // npg),
        in_specs=in_specs,
        out_specs=pl.BlockSpec((1, rows, LANE), per_b3),
        scratch_shapes=[pltpu.VMEM((rows, LANE), F32)] * 3,
    )
    return pl.pallas_call(
        kern,
        grid_spec=grid_spec,
        out_shape=jax.ShapeDtypeStruct((nb, rows, LANE), F32),
        compiler_params=_cparams("parallel", "arbitrary"),
        name="paged",
    )(page_table, *operands)


def _rot_cols(w, seg):
    k, n = w.shape[-2:]
    w4 = w.reshape(w.shape[:-1] + (n // seg, 2, seg // 2))
    return jnp.stack([-w4[..., 1, :], w4[..., 0, :]], axis=-2).reshape(w.shape)


def _rope_tables(pos, d):
    half = d // 2
    inv = ROPE_THETA ** (-jnp.arange(half, dtype=F32) * 2.0 / d)
    ang = pos.astype(F32)[:, None] * inv[None, :]
    cos = jnp.cos(ang)
    sin = jnp.sin(ang)
    return jnp.concatenate([cos, cos], axis=-1), jnp.concatenate([sin, sin], axis=-1)


def _block_diag(blocks):
    g, r, c = blocks.shape
    eye = jnp.eye(g, dtype=blocks.dtype)
    return jnp.einsum('grc,gh->grhc', blocks, eye).reshape(g * r, g * c)


def _cplx_pow2(ar, ai, n_sq):
    for _ in range(n_sq):
        ar, ai = ar * ar - ai * ai, 2.0 * ar * ai
    return ar, ai


def kernel(x_prompt, x_sample, cache_mla_latent, cache_mla_krope, cache_diff_k, cache_diff_v, state_ssm_re, state_ssm_im, page_table, g_mix, w_in, b_gate, ssm_lambda_re, ssm_lambda_im, ssm_log_dt, ssm_b_re, ssm_b_im, ssm_c_re, ssm_c_im, ssm_d, ssm_w_glu, ssm_b_glu, mla_g_q, mla_w_uq, mla_g_kv, mla_w_uk, mla_w_uv, diff_lambda_q1, diff_lambda_k1, diff_lambda_q2, diff_lambda_k2, diff_g_sub, w_br_ssm, w_br_mla, w_br_diff, w_out, g_mlp, w_up, w_down, g_final):
    nbp, seq, dm = x_prompt.shape
    nbs, steps, _ = x_sample.shape
    depth = w_in.shape[0]
    n_pool, page = cache_mla_latent.shape[1:3]
    past_len = page_table.shape[1] * page
    ssm_w = ssm_w_glu.shape[1]
    n_grp = ssm_w // SSM_GROUP
    ns = n_grp * SSM_STATE
    q_lora = mla_g_q.shape[1]
    kv_lora = mla_g_kv.shape[1]
    mla_qd = MLA_NOPE + MLA_ROPE
    mla_scale = 1.0 / math.sqrt(mla_qd)
    diff_scale = 1.0 / math.sqrt(DIFF_HD)
    dq_w = DIFF_HEADS * 2 * DIFF_HD
    dk_w = DIFF_KV_HEADS * 2 * DIFF_HD
    dv_w = DIFF_KV_HEADS * DIFF_VD
    n_br = b_gate.shape[1]
    tp = nbp * seq
    ts = nbs * steps
    tt = tp + ts

    tm = 768 if tt % 768 == 0 else 256
    tm_in = 256
    tq = min(256, seq)
    tk = min(512, seq)
    tl = min(512, seq)
    chunk_len = tl // SUBLANE

    sizes = (ssm_w, q_lora, kv_lora, MLA_ROPE, dq_w, dk_w, dv_w, n_br * dm)
    offs = [0]
    for s in sizes:
        offs.append(offs[-1] + s)
    col = lambda i: w_in[:, :, offs[i]:offs[i + 1]]
    w_kr = col(3)
    lane_pad = lambda w: jnp.concatenate([w, jnp.zeros(w.shape[:-1] + (LANE - w.shape[-1],), F32)], axis=-1)
    w_ext = jnp.concatenate([col(0), col(1), col(2), lane_pad(w_kr), lane_pad(_rot_cols(w_kr, MLA_ROPE)),
                             col(4), _rot_cols(col(4), DIFF_HD), col(5), _rot_cols(col(5), DIFF_HD),
                             col(6), col(7)], axis=-1).astype(BF16)
    e_sizes = (ssm_w, q_lora, kv_lora, LANE, LANE, dq_w, dq_w, dk_w, dk_w, dv_w, n_br * dm)
    e_offs = [0]
    for s in e_sizes:
        e_offs.append(e_offs[-1] + s)

    step = jnp.exp(ssm_log_dt)[..., None]
    mag = jnp.exp(ssm_lambda_re * step)
    a_re = mag * jnp.cos(ssm_lambda_im * step)
    a_im = mag * jnp.sin(ssm_lambda_im * step)
    den = ssm_lambda_re * ssm_lambda_re + ssm_lambda_im * ssm_lambda_im
    z_re = ((a_re - 1.0) * ssm_lambda_re + a_im * ssm_lambda_im) / den
    z_im = (a_im * ssm_lambda_re - (a_re - 1.0) * ssm_lambda_im) / den
    bb_re = z_re[..., None] * ssm_b_re - z_im[..., None] * ssm_b_im
    bb_im = z_re[..., None] * ssm_b_im + z_im[..., None] * ssm_b_re
    bd = jax.vmap(_block_diag)
    wb = jnp.concatenate([bd(jnp.swapaxes(bb_re, -1, -2)), bd(jnp.swapaxes(bb_im, -1, -2))],
                         axis=-1).astype(BF16)
    wc = jnp.concatenate([bd(jnp.swapaxes(ssm_c_re, -1, -2)), -bd(jnp.swapaxes(ssm_c_im, -1, -2))],
                         axis=-2).astype(BF16)
    a_vec = jnp.stack([a_re.reshape(depth, ns), a_im.reshape(depth, ns)], axis=1).reshape(depth, 1, 2 * ns)
    p_re, p_im = _cplx_pow2(a_re, a_im, int(math.log2(chunk_len)))
    a_pow = jnp.stack([p_re.reshape(depth, ns), p_im.reshape(depth, ns)], axis=1).reshape(depth, 1, 2 * ns)
    d_vec = ssm_d.reshape(depth, 1, ssm_w)
    w_glu = ssm_w_glu.astype(BF16)
    b_glu = ssm_b_glu.reshape(depth, 1, ssm_w)

    w_uq_nope = mla_w_uq.reshape(depth, q_lora, MLA_HEADS, mla_qd)[..., :MLA_NOPE].reshape(depth, q_lora, -1)
    w_uq_rope = mla_w_uq.reshape(depth, q_lora, MLA_HEADS, mla_qd)[..., MLA_NOPE:].reshape(depth, q_lora, -1)
    head_pad = lambda w: lane_pad(w.reshape(depth, q_lora, MLA_HEADS, MLA_ROPE)).reshape(depth, q_lora, -1)
    w_uq_ext = jnp.concatenate([w_uq_nope, head_pad(w_uq_rope), head_pad(_rot_cols(w_uq_rope, MLA_ROPE))],
                               axis=-1).astype(BF16)
    w_uk_bd = bd(jnp.transpose(mla_w_uk, (0, 2, 3, 1))).astype(BF16)
    w_uv_bd = bd(jnp.transpose(mla_w_uv, (0, 2, 1, 3))).astype(BF16)

    lam = (jnp.exp(jnp.sum(diff_lambda_q1 * diff_lambda_k1, axis=-1))
           - jnp.exp(jnp.sum(diff_lambda_q2 * diff_lambda_k2, axis=-1)))
    w_bs, w_bm, w_bd_, w_o = (w.astype(BF16) for w in (w_br_ssm, w_br_mla, w_br_diff, w_out))
    w_u, w_dn = w_up.astype(BF16), w_down.astype(BF16)
    bg = b_gate.reshape(depth, 1, n_br * dm)

    pos = jnp.concatenate([jnp.tile(jnp.arange(seq), nbp), jnp.tile(past_len + jnp.arange(steps), nbs)])
    cos32, sin32 = _rope_tables(pos, MLA_ROPE)
    cos64, sin64 = _rope_tables(pos, DIFF_HD)
    c64, s64 = jnp.tile(cos64, (1, LANE // DIFF_HD)), jnp.tile(sin64, (1, LANE // DIFF_HD))
    c32, s32 = lane_pad(cos32), lane_pad(sin32)

    cache_krt = jnp.transpose(cache_mla_krope, (0, 1, 3, 2))
    cache_dkt = jnp.transpose(cache_diff_k, (0, 1, 3, 4, 5, 2)).reshape(depth, n_pool, dk_w, page)
    cache_dv2 = cache_diff_v.reshape(depth, n_pool, page * DIFF_KV_HEADS, DIFF_VD)
    s0_grp = jnp.concatenate([state_ssm_re.reshape(depth, nbs, ns), state_ssm_im.reshape(depth, nbs, ns)],
                             axis=-1).reshape(depth, nbs // SUBLANE, SUBLANE, 2 * ns)

    x = jnp.concatenate([x_prompt.reshape(tp, dm), x_sample.reshape(ts, dm)], axis=0)
    rows = [[] for _ in range(12)]
    for l in range(depth):
        u, cq_n, ckv_n, kr128, kpad, qd_blk, kd, kd_bf, dv, dv_bf, gz = _in_proj(
            x, g_mix[l][None], w_ext[l], mla_g_q[l][None], mla_g_kv[l][None], c64, s64, c32, s32,
            e_offs, diff_scale * LOG2E, tm_in)
        kr = kr128[:, :MLA_ROPE]

        u_p = u[:tp].reshape(nbp, seq // tl, SUBLANE, chunk_len, ssm_w)
        u_p = jnp.swapaxes(u_p, 2, 3).reshape(nbp, seq, ssm_w)
        y_p, fin_p = _ssm_prompt(u_p, wb[l], a_vec[l], a_pow[l], wc[l], d_vec[l], w_glu[l], b_glu[l], tl)
        y_p = jnp.swapaxes(y_p.reshape(nbp, seq // tl, chunk_len, SUBLANE, ssm_w), 2, 3).reshape(tp, ssm_w)
        u_s = u[tp:].reshape(nbs // SUBLANE, SUBLANE, steps, ssm_w)
        u_s = jnp.swapaxes(u_s, 1, 2).reshape(nbs // SUBLANE, steps * SUBLANE, ssm_w)
        y_s, fin_s = _ssm_sample(u_s, s0_grp[l], wb[l], a_vec[l], wc[l], d_vec[l], w_glu[l], b_glu[l], steps)
        y_s = jnp.swapaxes(y_s.reshape(nbs // SUBLANE, steps, SUBLANE, ssm_w), 1, 2).reshape(ts, ssm_w)
        y_ssm = jnp.concatenate([y_p, y_s], axis=0)
        fin_p = fin_p.reshape(nbp, 2, n_grp, SSM_STATE)
        fin_s = fin_s.reshape(nbs, 2, n_grp, SSM_STATE)

        q_pad = _q_proj(cq_n, w_uq_ext[l], w_uk_bd[l], c32, s32, MLA_HEADS * MLA_NOPE, MLA_HEADS,
                        mla_scale * LOG2E, tm)
        no_par = jnp.zeros((SUBLANE, LANE), F32)
        o_p = _flash(q_pad, kpad, kpad, no_par, nbp, seq, MLA_HEADS, 1, tq, tk, False)

        lam_init = 0.8 - 0.6 * math.exp(-0.3 * l)
        lam_full = lam[l] + lam_init
        par = jnp.concatenate([jnp.full((1, LANE), lam_full, F32), diff_g_sub[l][None],
                               jnp.full((1, LANE), 1.0 - lam_init, F32),
                               jnp.zeros((SUBLANE - 3, LANE), F32)], axis=0)
        y_diff_p = _flash(qd_blk, kd_bf, dv_bf, par, nbp, seq, DIFF_GROUP * 2, DIFF_KV_HEADS, tq, tk, True)

        by_tok = lambda a: jnp.pad(a[tp:].reshape(nbs, steps, -1), ((0, 0), (0, -steps % SUBLANE), (0, 0)))
        q_s = jnp.transpose(q_pad[tp:].reshape(nbs, steps, MLA_HEADS, kv_lora + LANE), (0, 2, 1, 3))
        q_s = q_s.reshape(nbs, MLA_HEADS * steps, kv_lora + LANE)
        ql_s = q_s[:, :, :kv_lora]
        qr_s = q_s[:, :, kv_lora:kv_lora + MLA_ROPE]
        qd_s = jnp.transpose(qd_blk[tp:].reshape(nbs, steps, DIFF_KV_HEADS, DIFF_GROUP * 2, 2 * DIFF_HD),
                             (0, 2, 3, 1, 4)).reshape(nbs, DIFF_KV_HEADS, DIFF_GROUP * 2 * steps, 2 * DIFF_HD)
        o_all = _paged(page_table, l, ql_s, qr_s, qd_s, by_tok(ckv_n), by_tok(kr), by_tok(kd), by_tok(dv),
                       cache_mla_latent, cache_krt, cache_dkt, cache_dv2, steps)
        rows_m = MLA_HEADS * steps
        o_s = jnp.transpose(o_all[:, :rows_m].reshape(nbs, MLA_HEADS, steps, kv_lora), (0, 2, 1, 3))
        o_s = o_s.reshape(ts, MLA_HEADS * kv_lora)
        od_s = jnp.transpose(o_all[:, rows_m:].reshape(nbs, DIFF_KV_HEADS, DIFF_GROUP, 2, steps, DIFF_VD),
                             (0, 4, 1, 2, 3, 5))
        od_s = od_s.reshape(ts, DIFF_HEADS, 2, DIFF_VD)
        od_s = od_s[:, :, 0, :] - lam_full * od_s[:, :, 1, :]
        y_diff_s = (_rms(od_s, diff_g_sub[l]) * (1.0 - lam_init)).reshape(ts, DIFF_HEADS * DIFF_VD)

        o_mla = jnp.concatenate([o_p, o_s], axis=0)
        y_diff = jnp.concatenate([y_diff_p, y_diff_s], axis=0)
        x = _merge(x, y_ssm, o_mla, y_diff, gz, bg[l], w_uv_bd[l], w_bs[l], w_bm[l], w_bd_[l], w_o[l], tm)
        x = _mlp(x, g_mlp[l][None], w_u[l], w_dn[l], tm, 1024)

        new = (ckv_n, kr, kd, dv)
        shapes = ((kv_lora,), (MLA_ROPE,), (DIFF_KV_HEADS, 2, DIFF_HD), (DIFF_KV_HEADS, DIFF_VD))
        for i, (a, shp) in enumerate(zip(new, shapes)):
            rows[i].append(a[:tp].reshape((nbp, seq) + shp))
            rows[6 + i].append(a[tp:].reshape((nbs, steps) + shp))
        rows[4].append(fin_p[:, 0])
        rows[5].append(fin_p[:, 1])
        rows[10].append(fin_s[:, 0])
        rows[11].append(fin_s[:, 1])

    y = _norm(x, g_final[None], tm)
    outs = [jnp.stack(r) for r in rows]
    return (y[:tp].reshape(nbp, seq, dm), y[tp:].reshape(nbs, steps, dm), *outs)
```
